```python
import jax, jax.numpy as jnp
from jax import lax
import numpy as np

D_MODEL = 2048
BATCH = 8
SEQ = 2048
DEPTH = 2

HEAD_DIM = 128
A_GROUPS = 4
A_WIDTH = A_GROUPS * HEAD_DIM
A_CHUNK = 128
B_HEADS = 6
B_WIDTH = B_HEADS * HEAD_DIM
B_CONV = 4
B_CHUNK = 64
C_HEADS = 6
C_WIDTH = C_HEADS * HEAD_DIM
C_BRANCHES = ((128, 1), (512, 4), (2048, 16))
C_BLOCK = 128
MIX_WIDTH = A_WIDTH + B_WIDTH + C_WIDTH
FFN_HIDDEN = -(-8 * D_MODEL // (3 * 256)) * 256
IN_SIZES = (A_WIDTH, A_WIDTH,
            B_WIDTH, B_WIDTH, B_WIDTH, B_WIDTH, B_HEADS, B_HEADS,
            C_WIDTH, C_WIDTH, C_WIDTH)
IN_TOTAL = sum(IN_SIZES)
EPS = 1e-6

kernel_name = "hybrid_sgu_deltanet_dilated_attn"


def rms_norm(x, g):
    xf = x.astype(jnp.float32)
    y = xf * lax.rsqrt(jnp.mean(xf * xf, axis=-1, keepdims=True) + EPS)
    return (y * g.astype(jnp.float32)).astype(x.dtype)


def l2_norm(x):
    return x * lax.rsqrt(jnp.sum(x * x, axis=-1, keepdims=True) + EPS)


def chunked_spatial_gating(u, v, sgu_g, w_s, b_s):
    bsz, s, _ = u.shape
    nc = s // A_CHUNK
    u = jax.nn.gelu(u)
    v = rms_norm(jax.nn.gelu(v).reshape(bsz, s, A_GROUPS, HEAD_DIM), sgu_g)
    v = v.reshape(bsz, nc, A_CHUNK, A_GROUPS, HEAD_DIM)
    causal = jnp.tril(jnp.ones((A_CHUNK, A_CHUNK), dtype=bool))
    w = jnp.where(causal[None], w_s, jnp.zeros_like(w_s)).astype(v.dtype)
    z = jnp.einsum('gij,bcjgd->bcigd', w, v) + b_s.T.astype(v.dtype)[None, None, :, :, None]
    return u * z.reshape(bsz, s, A_WIDTH)


def causal_depthwise_conv(x, w):
    k = w.shape[0]
    s = x.shape[1]
    xp = jnp.pad(x, ((0, 0), (k - 1, 0), (0, 0)))
    return sum(xp[:, j:j + s] * w[j].astype(x.dtype) for j in range(k))


def gated_delta_net(q, k, v, gate, beta_logit, a, conv_w, a_log, dt_bias, o_norm_g):
    bsz, s, _ = q.shape
    f32 = jnp.float32
    qkv = jax.nn.silu(causal_depthwise_conv(jnp.concatenate([q, k, v], axis=-1), conv_w))
    q, k, v = jnp.split(qkv, 3, axis=-1)
    to_heads = lambda t: t.reshape(bsz, s, B_HEADS, HEAD_DIM).transpose(0, 2, 1, 3).astype(f32)
    q = l2_norm(to_heads(q)) * (HEAD_DIM ** -0.5)
    k = l2_norm(to_heads(k))
    v = to_heads(v)
    beta = jax.nn.sigmoid(beta_logit.astype(f32)).transpose(0, 2, 1)
    g = (-jnp.exp(a_log.astype(f32))[None, None, :]
         * jax.nn.softplus(a.astype(f32) + dt_bias.astype(f32)[None, None, :])).transpose(0, 2, 1)
    nc = s // B_CHUNK
    chunk = lambda t: t.reshape(t.shape[:2] + (nc, B_CHUNK) + t.shape[3:])
    q, k, v, beta, g = map(chunk, (q, k, v, beta, g))
    g = jnp.cumsum(g, axis=-1)
    causal = jnp.tril(jnp.ones((B_CHUNK, B_CHUNK), dtype=bool))
    strict = jnp.tril(jnp.ones((B_CHUNK, B_CHUNK), dtype=bool), -1)
    gdiff = g[..., :, None] - g[..., None, :]
    decay_mat = jnp.exp(jnp.where(causal, gdiff, -jnp.inf))
    k_beta = k * beta[..., None]
    a_mat = jnp.einsum('bhnid,bhnjd->bhnij', k_beta, k) * jnp.where(strict, decay_mat, 0.0)
    m = a_mat + jnp.eye(B_CHUNK, dtype=f32)
    rhs = jnp.concatenate([v * beta[..., None], k_beta * jnp.exp(g)[..., None]], axis=-1)
    sol = lax.linalg.triangular_solve(m, rhs, left_side=True, lower=True, unit_diagonal=True)
    u_vals, w_keys = jnp.split(sol, 2, axis=-1)
    attn_intra = jnp.einsum('bhnid,bhnjd->bhnij', q, k) * decay_mat

    def step(state, xs):
        q_c, k_c, u_c, w_c, g_c, attn_c = xs
        v_new = u_c - jnp.einsum('bhik,bhkv->bhiv', w_c, state)
        o_c = (jnp.einsum('bhik,bhkv->bhiv', q_c * jnp.exp(g_c)[..., None], state)
               + jnp.einsum('bhij,bhjv->bhiv', attn_c, v_new))
        g_last = g_c[..., -1:]
        state = (state * jnp.exp(g_last)[..., None]
                 + jnp.einsum('bhik,bhiv->bhkv', k_c * jnp.exp(g_last - g_c)[..., None], v_new))
        return state, o_c

    xs = tuple(jnp.moveaxis(t, 2, 0) for t in (q, k, u_vals, w_keys, g, attn_intra))
    state0 = jnp.zeros((bsz, B_HEADS, HEAD_DIM, HEAD_DIM), f32)
    _, o = lax.scan(step, state0, xs)
    o = jnp.moveaxis(o, 0, 2).reshape(bsz, B_HEADS, s, HEAD_DIM).transpose(0, 2, 1, 3)
    gate = gate.reshape(bsz, s, B_HEADS, HEAD_DIM)
    y = rms_norm(o, o_norm_g) * jax.nn.silu(gate.astype(f32))
    return y.astype(gate.dtype).reshape(bsz, s, B_WIDTH)


def dilated_branch(q, k, v, slopes, window, dilation):
    bsz, s, h, hd = q.shape
    span = window // dilation
    seg = s // dilation
    nb = -(-seg // C_BLOCK)
    lp = nb * C_BLOCK

    def to_blocks(t):
        t = t.reshape(bsz, seg, dilation, h, hd)
        t = jnp.pad(t, ((0, 0), (0, lp - seg), (0, 0), (0, 0), (0, 0)))
        return t.reshape(bsz, nb, C_BLOCK, dilation, h, hd)

    qb, kb, vb = map(to_blocks, (q, k, v))
    prev = lambda t: jnp.pad(t, ((0, 0), (1, 0), (0, 0), (0, 0), (0, 0), (0, 0)))[:, :-1]
    kb = jnp.concatenate([prev(kb), kb], axis=2)
    vb = jnp.concatenate([prev(vb), vb], axis=2)
    scores = jnp.einsum('bnqrhd,bnkrhd->bnrhqk', qb, kb).astype(jnp.float32) * (hd ** -0.5)
    qi = jnp.arange(C_BLOCK)[:, None]
    kj = jnp.arange(2 * C_BLOCK)[None, :]
    delta = C_BLOCK + qi - kj
    in_band = (delta >= 0) & (delta <= span)
    key_exists = (jnp.arange(nb)[:, None, None] > 0) | (kj >= C_BLOCK)[None]
    mask = in_band[None] & key_exists
    bias = -slopes[:, None, None] * (delta * dilation).astype(jnp.float32)[None]
    sc = jnp.where(mask[None, :, None, None], scores + bias[None, None, None], -jnp.inf)
    mx = jnp.max(sc, axis=-1, keepdims=True)
    p = jnp.exp(sc - mx)
    den = jnp.sum(p, axis=-1, keepdims=True)
    out = jnp.einsum('bnrhqk,bnkrhd->bnqrhd', (p / den).astype(v.dtype), vb)
    lse = (mx + jnp.log(den))[..., 0]
    out = out.reshape(bsz, lp, dilation, h, hd)[:, :seg].reshape(bsz, s, h, hd)
    lse = lse.transpose(0, 1, 4, 2, 3).reshape(bsz, lp, dilation, h)[:, :seg].reshape(bsz, s, h)
    return out, lse


def dilated_attention(q, k, v, q_g, k_g, slopes):
    bsz, s, _ = q.shape
    to_heads = lambda t: t.reshape(bsz, s, C_HEADS, HEAD_DIM)
    q = rms_norm(to_heads(q), q_g)
    k = rms_norm(to_heads(k), k_g)
    v = to_heads(v)
    outs, lses = zip(*(dilated_branch(q, k, v, slopes, w, r) for (w, r) in C_BRANCHES))
    wts = jax.nn.softmax(jnp.stack(lses, axis=0), axis=0)
    out = jnp.sum(wts[..., None] * jnp.stack(outs, axis=0).astype(jnp.float32), axis=0)
    return out.astype(v.dtype).reshape(bsz, s, C_WIDTH)


def setup_inputs(seed: int = 0) -> dict:
    key = jax.random.key(seed)
    ks = jax.random.split(key, 16)
    f32 = jnp.float32
    nrm = lambda k, shape, scale: jax.random.normal(k, shape, f32) * scale
    gain = lambda k, shape: 1.0 + 0.02 * jax.random.normal(k, shape, f32)
    dt = jnp.exp(jax.random.uniform(ks[7], (DEPTH, B_HEADS), f32, np.log(1e-3), np.log(1e-1)))
    return {
        "x": jax.random.normal(ks[0], (BATCH, SEQ, D_MODEL), f32),
        "norm1_g": gain(ks[1], (DEPTH, D_MODEL)),
        "w_in": nrm(ks[2], (DEPTH, D_MODEL, IN_TOTAL), D_MODEL ** -0.5),
        "sgu_norm_g": gain(ks[3], (DEPTH, A_GROUPS, HEAD_DIM)),
        "w_spatial": nrm(ks[4], (DEPTH, A_GROUPS, A_CHUNK, A_CHUNK), A_CHUNK ** -0.5),
        "b_spatial": gain(ks[5], (DEPTH, A_GROUPS, A_CHUNK)),
        "conv_w": nrm(ks[6], (DEPTH, B_CONV, 3 * B_WIDTH), B_CONV ** -0.5),
        "a_log": jnp.log(jax.random.uniform(ks[8], (DEPTH, B_HEADS), f32, 1.0, 16.0)),
        "dt_bias": dt + jnp.log(-jnp.expm1(-dt)),
        "o_norm_g": gain(ks[9], (DEPTH, HEAD_DIM)),
        "q_norm_g": gain(ks[10], (DEPTH, HEAD_DIM)),
        "k_norm_g": gain(ks[11], (DEPTH, HEAD_DIM)),
        "w_out": nrm(ks[12], (DEPTH, MIX_WIDTH, D_MODEL), MIX_WIDTH ** -0.5),
        "norm2_g": gain(ks[13], (DEPTH, D_MODEL)),
        "w_gate_up": nrm(ks[14], (DEPTH, D_MODEL, 2 * FFN_HIDDEN), D_MODEL ** -0.5),
        "w_down": nrm(ks[15], (DEPTH, FFN_HIDDEN, D_MODEL), FFN_HIDDEN ** -0.5),
    }


def reference(x, norm1_g, w_in, sgu_norm_g, w_spatial, b_spatial, conv_w, a_log, dt_bias,
              o_norm_g, q_norm_g, k_norm_g, w_out, norm2_g, w_gate_up, w_down):
    slopes = jnp.exp2(-8.0 * (jnp.arange(C_HEADS, dtype=jnp.float32) + 1.0) / C_HEADS)
    offsets = np.cumsum(IN_SIZES)[:-1].tolist()
    for layer in range(DEPTH):
        h = rms_norm(x, norm1_g[layer])
        proj = jnp.einsum('bsd,de->bse', h, w_in[layer])
        (a_u, a_v, b_q, b_k, b_v, b_gate, b_beta, b_a, c_q, c_k, c_v) = jnp.split(proj, offsets, axis=-1)
        y_a = chunked_spatial_gating(a_u, a_v, sgu_norm_g[layer], w_spatial[layer], b_spatial[layer])
        y_b = gated_delta_net(b_q, b_k, b_v, b_gate, b_beta, b_a, conv_w[layer], a_log[layer],
                              dt_bias[layer], o_norm_g[layer])
        y_c = dilated_attention(c_q, c_k, c_v, q_norm_g[layer], k_norm_g[layer], slopes)
        mix = jnp.concatenate([y_a, y_b, y_c], axis=-1)
        x = x + jnp.einsum('bse,ed->bsd', mix, w_out[layer])
        h = rms_norm(x, norm2_g[layer])
        gt, up = jnp.split(jnp.einsum('bsd,df->bsf', h, w_gate_up[layer]), 2, axis=-1)
        x = x + jnp.einsum('bsf,fd->bsd', jax.nn.silu(gt) * up, w_down[layer])
    return x
```

```python
import functools

import jax
import jax.numpy as jnp
from jax import lax
from jax.experimental import pallas as pl
from jax.experimental.pallas import tpu as pltpu

F32 = jnp.float32
BF16 = jnp.bfloat16

D_MODEL = 2048
HEAD_DIM = 128
A_GROUPS = 4
A_WIDTH = A_GROUPS * HEAD_DIM
A_CHUNK = 128
B_HEADS = 6
B_WIDTH = B_HEADS * HEAD_DIM
B_CONV = 4
C_HEADS = 6
C_WIDTH = C_HEADS * HEAD_DIM
C_BRANCHES = ((128, 1), (512, 4), (2048, 16))
C_BLOCK = 128
FFN_HIDDEN = 5632
EPS = 1e-6

OFF_B = 2 * A_WIDTH
OFF_BETA = OFF_B + 4 * B_WIDTH
OFF_C = OFF_BETA + 2 * B_HEADS
IN_TOTAL = OFF_C + 3 * C_WIDTH
SLAB_A_U, SLAB_A_V = 0, 4
SLAB_B_Q, SLAB_B_K, SLAB_B_V, SLAB_B_G = 8, 14, 20, 26
SLAB_C_Q, SLAB_C_K, SLAB_C_V = 32, 38, 44
N_SLABS = 50
SMALL_ROWS = 16

LANES = 128
DN_CHUNK = 128
VMEM_LIMIT = 56 * 1024 * 1024

NT_DIMS = (((1,), (1,)), ((), ()))


def _mm(a, b):
    return jnp.dot(a.astype(BF16), b.astype(BF16), preferred_element_type=F32)


def _mm_nt(a, b):
    return lax.dot_general(a.astype(BF16), b.astype(BF16), NT_DIMS, preferred_element_type=F32)


def _sigmoid(x):
    return 1.0 / (1.0 + jnp.exp(-x))


def _silu(x):
    return x * _sigmoid(x)


def _gelu_tanh(x):
    return 0.5 * x * (1.0 + jnp.tanh(0.7978845608028654 * (x + 0.044715 * (x * x * x))))


def _rms(x, gain):
    return x * lax.rsqrt(jnp.mean(x * x, axis=-1, keepdims=True) + EPS) * gain


def _params(*sem):
    return pltpu.CompilerParams(dimension_semantics=sem, vmem_limit_bytes=VMEM_LIMIT)


def _in_proj_kernel(x_ref, g_ref, w_ref, ws_ref, o_ref, os_ref, h_scr, *, slabs_per_step):
    @pl.when(pl.program_id(1) == 0)
    def _():
        h = _rms(x_ref[...], g_ref[...]).astype(BF16)
        h_scr[...] = h
        os_ref[...] = lax.dot_general(ws_ref[...], h, NT_DIMS, preferred_element_type=F32)

    res = jnp.dot(h_scr[...], w_ref[...], preferred_element_type=F32)
    for c in range(slabs_per_step):
        o_ref[c] = res[:, c * LANES:(c + 1) * LANES].astype(BF16)


def _in_proj(x2, gain, w, ws, *, tm=512, slabs_per_step=10):
    t = x2.shape[0]
    tn = slabs_per_step * LANES
    return pl.pallas_call(
        functools.partial(_in_proj_kernel, slabs_per_step=slabs_per_step),
        grid=(t // tm, N_SLABS // slabs_per_step),
        in_specs=[
            pl.BlockSpec((tm, D_MODEL), lambda i, j: (i, 0)),
            pl.BlockSpec((1, D_MODEL), lambda i, j: (0, 0)),
            pl.BlockSpec((D_MODEL, tn), lambda i, j: (0, j)),
            pl.BlockSpec((SMALL_ROWS, D_MODEL), lambda i, j: (0, 0)),
        ],
        out_specs=[
            pl.BlockSpec((slabs_per_step, tm, LANES), lambda i, j: (j, i, 0)),
            pl.BlockSpec((SMALL_ROWS, tm), lambda i, j: (0, i)),
        ],
        out_shape=[
            jax.ShapeDtypeStruct((N_SLABS, t, LANES), BF16),
            jax.ShapeDtypeStruct((SMALL_ROWS, t), F32),
        ],
        scratch_shapes=[pltpu.VMEM((tm, D_MODEL), BF16)],
        compiler_params=_params("parallel", "arbitrary"),
        name="in_proj",
    )(x2, gain, w, ws)


def _mixer_a_kernel(p_ref, sg_ref, ws_ref, bs_ref, o_ref, *, chunks):
    ii = lax.broadcasted_iota(jnp.int32, (A_CHUNK, A_CHUNK), 0)
    jj = lax.broadcasted_iota(jnp.int32, (A_CHUNK, A_CHUNK), 1)
    for g in range(A_GROUPS):
        w = jnp.where(ii >= jj, ws_ref[g], 0.0).astype(BF16)
        gain = sg_ref[g:g + 1, :]
        bias = bs_ref[g]
        for c in range(chunks):
            rows = slice(c * A_CHUNK, (c + 1) * A_CHUNK)
            u = _gelu_tanh(p_ref[SLAB_A_U + g, rows, :].astype(F32))
            v = _rms(_gelu_tanh(p_ref[SLAB_A_V + g, rows, :].astype(F32)), gain)
            z = jnp.dot(w, v.astype(BF16), preferred_element_type=F32) + bias
            o_ref[g, rows, :] = (u * z).astype(BF16)


def _mixer_a(proj, sgu_g, w_s, b_s, *, tm=512):
    t = proj.shape[1]
    return pl.pallas_call(
        functools.partial(_mixer_a_kernel, chunks=tm // A_CHUNK),
        grid=(t // tm,),
        in_specs=[
            pl.BlockSpec((2 * A_GROUPS, tm, LANES), lambda i: (0, i, 0)),
            pl.BlockSpec((A_GROUPS, HEAD_DIM), lambda i: (0, 0)),
            pl.BlockSpec((A_GROUPS, A_CHUNK, A_CHUNK), lambda i: (0, 0, 0)),
            pl.BlockSpec((A_GROUPS, A_CHUNK, 1), lambda i: (0, 0, 0)),
        ],
        out_specs=pl.BlockSpec((A_GROUPS, tm, LANES), lambda i: (0, i, 0)),
        out_shape=jax.ShapeDtypeStruct((A_GROUPS, t, LANES), BF16),
        compiler_params=_params("parallel"),
        name="mixer_a",
    )(proj, sgu_g, w_s, b_s.reshape(A_GROUPS, A_CHUNK, 1))


def _unit_lower_inverse(a, ii, jj):
    def same_block(shift):
        return (ii >> shift) == (jj >> shift)

    a_d = jnp.where(same_block(4), a, 0.0)
    p = jnp.where(ii == jj, 1.0, 0.0) - a_d
    a_pow = a_d
    for _ in range(3):
        a_pow = _mm(a_pow, a_pow)
        p = p + _mm(p, a_pow)
    for shift in (4, 5, 6):
        a_off = jnp.where(same_block(shift + 1) & jnp.logical_not(same_block(shift)), a, 0.0)
        p = p - _mm(p, _mm(a_off, p))
    return p


def _mixer_b_kernel(alog_ref, dtb_ref, q_ref, k_ref, v_ref, gate_ref, beta_ref, a_ref,
                    cq_ref, ck_ref, cv_ref, og_ref, o_ref,
                    xpad, qn, kn, vn, gc_scr, bt_scr, el_scr, u_scr, w_scr, at_scr, qg_scr, kdt_scr,
                    *, seq):
    h = pl.program_id(1)
    nchunks = seq // DN_CHUNK
    pad = 8

    xpad[0:pad, :] = jnp.zeros((pad, HEAD_DIM), F32)
    for src, cw_ref, dst, kind in ((q_ref, cq_ref, qn, "q"), (k_ref, ck_ref, kn, "k"), (v_ref, cv_ref, vn, "v")):
        xpad[pad:pad + seq, :] = src[...].astype(F32)
        cw = cw_ref[...]
        for t in range(nchunks):
            base = pad + t * DN_CHUNK
            y = cw[B_CONV - 1:B_CONV, :] * xpad[base:base + DN_CHUNK, :]
            for s in range(1, B_CONV):
                y = y + cw[B_CONV - 1 - s:B_CONV - s, :] * xpad[base - s:base - s + DN_CHUNK, :]
            y = _silu(y)
            if kind != "v":
                y = y * lax.rsqrt(jnp.sum(y * y, axis=-1, keepdims=True) + EPS)
            if kind == "q":
                y = y * (HEAD_DIM ** -0.5)
            dst[t * DN_CHUNK:(t + 1) * DN_CHUNK, :] = y

    xa = a_ref[...] + dtb_ref[h]
    softplus = jnp.maximum(xa, 0.0) + jnp.log(1.0 + jnp.exp(-jnp.abs(xa)))
    g = -jnp.exp(alog_ref[h]) * softplus
    lane = lax.broadcasted_iota(jnp.int32, g.shape, 1)
    shift = 1
    while shift < DN_CHUNK:
        g = g + jnp.where(lane >= shift, pltpu.roll(g, shift, 1), 0.0)
        shift *= 2
    gc_scr[...] = g
    bt_scr[...] = _sigmoid(beta_ref[...])

    ii = lax.broadcasted_iota(jnp.int32, (DN_CHUNK, DN_CHUNK), 0)
    jj = lax.broadcasted_iota(jnp.int32, (DN_CHUNK, DN_CHUNK), 1)

    def prep(c, carry):
        rows = pl.ds(pl.multiple_of(c * DN_CHUNK, DN_CHUNK), DN_CHUNK)
        q = qn[rows, :]
        k = kn[rows, :]
        v = vn[rows, :]
        g_j = jnp.broadcast_to(gc_scr[pl.ds(c, 1), :], (DN_CHUNK, DN_CHUNK))
        b_j = jnp.broadcast_to(bt_scr[pl.ds(c, 1), :], (DN_CHUNK, DN_CHUNK))
        g_i = g_j.T
        b_i = b_j.T
        decay = jnp.exp(jnp.where(ii >= jj, g_i - g_j, -jnp.inf))
        kb = k * b_i
        a = jnp.where(ii > jj, _mm_nt(kb, k) * decay, 0.0)
        t_inv = _unit_lower_inverse(a, ii, jj)
        e_i = jnp.exp(g_i)
        sol = _mm(t_inv, jnp.concatenate([v * b_i, kb * e_i], axis=1))
        g_last = jnp.broadcast_to(g_j[:, DN_CHUNK - 1:DN_CHUNK], (DN_CHUNK, DN_CHUNK))
        u_scr[rows, :] = sol[:, :HEAD_DIM]
        w_scr[rows, :] = sol[:, HEAD_DIM:].astype(BF16)
        at_scr[rows, :] = (_mm_nt(q, k) * decay).astype(BF16)
        qg_scr[rows, :] = (q * e_i).astype(BF16)
        kdt_scr[rows, :] = (k * jnp.exp(g_last - g_i)).T.astype(BF16)
        el_scr[pl.ds(c, 1), :] = jnp.exp(g_last[0:1, :])
        return carry

    lax.fori_loop(0, nchunks, prep, 0)

    def scan(c, state):
        rows = pl.ds(pl.multiple_of(c * DN_CHUNK, DN_CHUNK), DN_CHUNK)
        sb = state.astype(BF16)
        v_new = u_scr[rows, :] - jnp.dot(w_scr[rows, :], sb, preferred_element_type=F32)
        vb = v_new.astype(BF16)
        o = (jnp.dot(qg_scr[rows, :], sb, preferred_element_type=F32)
             + jnp.dot(at_scr[rows, :], vb, preferred_element_type=F32))
        y = _rms(o, og_ref[...]) * _silu(gate_ref[rows, :].astype(F32))
        o_ref[rows, :] = y.astype(BF16)
        return state * el_scr[pl.ds(c, 1), :] + jnp.dot(kdt_scr[rows, :], vb, preferred_element_type=F32)

    lax.fori_loop(0, nchunks, scan, jnp.zeros((HEAD_DIM, HEAD_DIM), F32))


def _mixer_b(proj, small3, conv_w3, a_log, dt_bias, o_norm_g, *, batch, seq):
    t = proj.shape[1]
    nch = seq // DN_CHUNK
    slab = lambda s0: pl.BlockSpec((None, seq, LANES), lambda b, h: (s0 + h, b, 0))
    small = lambda r0: pl.BlockSpec((None, nch, LANES), lambda b, h: (r0 + h, b, 0))
    convw = lambda r0: pl.BlockSpec((None, B_CONV, LANES), lambda b, h: (r0 + h, 0, 0))
    smem = pl.BlockSpec(memory_space=pltpu.SMEM)
    return pl.pallas_call(
        functools.partial(_mixer_b_kernel, seq=seq),
        grid=(batch, B_HEADS),
        in_specs=[
            smem, smem,
            slab(SLAB_B_Q), slab(SLAB_B_K), slab(SLAB_B_V), slab(SLAB_B_G),
            small(0), small(B_HEADS),
            convw(0), convw(B_HEADS), convw(2 * B_HEADS),
            pl.BlockSpec((1, HEAD_DIM), lambda b, h: (0, 0)),
        ],
        out_specs=pl.BlockSpec((None, seq, LANES), lambda b, h: (h, b, 0)),
        out_shape=jax.ShapeDtypeStruct((B_HEADS, t, LANES), BF16),
        scratch_shapes=[
            pltpu.VMEM((seq + 8, HEAD_DIM), F32),
            pltpu.VMEM((seq, HEAD_DIM), F32),
            pltpu.VMEM((seq, HEAD_DIM), F32),
            pltpu.VMEM((seq, HEAD_DIM), F32),
            pltpu.VMEM((nch, DN_CHUNK), F32),
            pltpu.VMEM((nch, DN_CHUNK), F32),
            pltpu.VMEM((nch, LANES), F32),
            pltpu.VMEM((seq, HEAD_DIM), F32),
            pltpu.VMEM((seq, HEAD_DIM), BF16),
            pltpu.VMEM((seq, DN_CHUNK), BF16),
            pltpu.VMEM((seq, HEAD_DIM), BF16),
            pltpu.VMEM((seq, DN_CHUNK), BF16),
        ],
        compiler_params=_params("parallel", "arbitrary"),
        name="mixer_b",
    )(a_log, dt_bias, proj, proj, proj, proj, small3, small3, conv_w3, conv_w3, conv_w3, o_norm_g)


def _mixer_c_kernel(slope_ref, q_ref, k_ref, v_ref, qg_ref, kg_ref, o_ref,
                    qs, ks, vs, ob, lb, *, seq):
    h = pl.program_id(1)
    slope = slope_ref[h]
    qs[...] = _rms(q_ref[...].astype(F32), qg_ref[...])
    ks[...] = _rms(k_ref[...].astype(F32), kg_ref[...])
    vs[...] = v_ref[...].astype(F32)

    qi = lax.broadcasted_iota(jnp.int32, (C_BLOCK, 2 * C_BLOCK), 0)
    kj = lax.broadcasted_iota(jnp.int32, (C_BLOCK, 2 * C_BLOCK), 1)
    delta = C_BLOCK + qi - kj
    scale = HEAD_DIM ** -0.5

    for br, (window, dil) in enumerate(C_BRANCHES):
        span = window // dil
        seg = seq // dil
        in_band = (delta >= 0) & (delta <= span)
        bias = jnp.where(in_band, -slope * (delta * dil).astype(F32), -jnp.inf)
        bias_cur = bias[:, C_BLOCK:]

        def rows(start, size, dil=dil):
            return pl.ds(start, size, stride=dil) if dil > 1 else pl.ds(start, size)

        def attend(q_rows, k_rows, b, br=br):
            s = _mm_nt(qs[q_rows, :], ks[k_rows, :]) * scale + b
            mx = jnp.max(s, axis=-1, keepdims=True)
            p = jnp.exp(s - mx)
            den = jnp.sum(p, axis=-1, keepdims=True)
            ob[br, q_rows, :] = _mm(p / den, vs[k_rows, :])
            lb[br, q_rows, :] = jnp.broadcast_to(mx + jnp.log(den), (C_BLOCK, LANES))

        nb = seg // C_BLOCK
        if dil == 1:
            attend(rows(0, C_BLOCK), rows(0, C_BLOCK), bias_cur)

            def body(n, carry):
                q0 = pl.multiple_of(n * C_BLOCK, C_BLOCK)
                k0 = pl.multiple_of((n - 1) * C_BLOCK, C_BLOCK)
                attend(pl.ds(q0, C_BLOCK), pl.ds(k0, 2 * C_BLOCK), bias)
                return carry

            lax.fori_loop(1, nb, body, 0)
        else:
            for r in range(dil):
                attend(rows(r, C_BLOCK), rows(r, C_BLOCK), bias_cur)
                for n in range(1, nb):
                    attend(rows(r + n * C_BLOCK * dil, C_BLOCK),
                           rows(r + (n - 1) * C_BLOCK * dil, 2 * C_BLOCK), bias)

    def merge(n, carry):
        r = pl.ds(pl.multiple_of(n * C_BLOCK, C_BLOCK), C_BLOCK)
        l0, l1, l2 = lb[0, r, :], lb[1, r, :], lb[2, r, :]
        mx = jnp.maximum(jnp.maximum(l0, l1), l2)
        w0, w1, w2 = jnp.exp(l0 - mx), jnp.exp(l1 - mx), jnp.exp(l2 - mx)
        out = (w0 * ob[0, r, :] + w1 * ob[1, r, :] + w2 * ob[2, r, :]) / (w0 + w1 + w2)
        o_ref[r, :] = out.astype(BF16)
        return carry

    lax.fori_loop(0, seq // C_BLOCK, merge, 0)


def _mixer_c(proj, slopes, q_g, k_g, *, batch, seq):
    t = proj.shape[1]
    slab = lambda s0: pl.BlockSpec((None, seq, LANES), lambda b, h: (s0 + h, b, 0))
    vec = pl.BlockSpec((1, HEAD_DIM), lambda b, h: (0, 0))
    return pl.pallas_call(
        functools.partial(_mixer_c_kernel, seq=seq),
        grid=(batch, C_HEADS),
        in_specs=[pl.BlockSpec(memory_space=pltpu.SMEM),
                  slab(SLAB_C_Q), slab(SLAB_C_K), slab(SLAB_C_V), vec, vec],
        out_specs=pl.BlockSpec((None, seq, LANES), lambda b, h: (h, b, 0)),
        out_shape=jax.ShapeDtypeStruct((C_HEADS, t, LANES), BF16),
        scratch_shapes=[
            pltpu.VMEM((seq, HEAD_DIM), F32),
            pltpu.VMEM((seq, HEAD_DIM), F32),
            pltpu.VMEM((seq, HEAD_DIM), F32),
            pltpu.VMEM((len(C_BRANCHES), seq, HEAD_DIM), F32),
            pltpu.VMEM((len(C_BRANCHES), seq, LANES), F32),
        ],
        compiler_params=_params("parallel", "arbitrary"),
        name="mixer_c",
    )(slopes, proj, proj, proj, q_g, k_g)


def _out_proj_kernel(x_ref, ya_ref, yb_ref, yc_ref, w_ref, g_ref, xo_ref, ho_ref):
    mix = jnp.concatenate([ya_ref[i] for i in range(A_GROUPS)]
                          + [yb_ref[i] for i in range(B_HEADS)]
                          + [yc_ref[i] for i in range(C_HEADS)], axis=1)
    xn = x_ref[...] + jnp.dot(mix, w_ref[...], preferred_element_type=F32)
    xo_ref[...] = xn
    ho_ref[...] = _rms(xn, g_ref[...]).astype(BF16)


def _out_proj(x2, ya, yb, yc, w, gain, *, tm=512):
    t = x2.shape[0]
    return pl.pallas_call(
        _out_proj_kernel,
        grid=(t // tm,),
        in_specs=[
            pl.BlockSpec((tm, D_MODEL), lambda i: (i, 0)),
            pl.BlockSpec((A_GROUPS, tm, LANES), lambda i: (0, i, 0)),
            pl.BlockSpec((B_HEADS, tm, LANES), lambda i: (0, i, 0)),
            pl.BlockSpec((C_HEADS, tm, LANES), lambda i: (0, i, 0)),
            pl.BlockSpec((D_MODEL, D_MODEL), lambda i: (0, 0)),
            pl.BlockSpec((1, D_MODEL), lambda i: (0, 0)),
        ],
        out_specs=[
            pl.BlockSpec((tm, D_MODEL), lambda i: (i, 0)),
            pl.BlockSpec((tm, D_MODEL), lambda i: (i, 0)),
        ],
        out_shape=[
            jax.ShapeDtypeStruct((t, D_MODEL), F32),
            jax.ShapeDtypeStruct((t, D_MODEL), BF16),
        ],
        compiler_params=_params("parallel"),
        name="out_proj",
    )(x2, ya, yb, yc, w, gain)


def _ffn_kernel(h_ref, x_ref, wg_ref, wu_ref, wd_ref, o_ref):
    @pl.when(pl.program_id(1) == 0)
    def _():
        o_ref[...] = x_ref[...]

    h = h_ref[...]
    gt = jnp.dot(h, wg_ref[...], preferred_element_type=F32)
    up = jnp.dot(h, wu_ref[...], preferred_element_type=F32)
    act = (_silu(gt) * up).astype(BF16)
    o_ref[...] += jnp.dot(act, wd_ref[...], preferred_element_type=F32)


def _ffn(h2, x2, w_gate_up, w_down, *, tm=512, tf=512):
    t = x2.shape[0]
    nf = FFN_HIDDEN // tf
    return pl.pallas_call(
        _ffn_kernel,
        grid=(t // tm, nf),
        in_specs=[
            pl.BlockSpec((tm, D_MODEL), lambda i, j: (i, 0)),
            pl.BlockSpec((tm, D_MODEL), lambda i, j: (i, 0)),
            pl.BlockSpec((D_MODEL, tf), lambda i, j: (0, j)),
            pl.BlockSpec((D_MODEL, tf), lambda i, j: (0, nf + j)),
            pl.BlockSpec((tf, D_MODEL), lambda i, j: (j, 0)),
        ],
        out_specs=pl.BlockSpec((tm, D_MODEL), lambda i, j: (i, 0)),
        out_shape=jax.ShapeDtypeStruct((t, D_MODEL), F32),
        compiler_params=_params("parallel", "arbitrary"),
        name="ffn",
    )(h2, x2, w_gate_up, w_gate_up, w_down)


def kernel(x, norm1_g, w_in, sgu_norm_g, w_spatial, b_spatial, conv_w, a_log, dt_bias, o_norm_g,
           q_norm_g, k_norm_g, w_out, norm2_g, w_gate_up, w_down):
    batch, seq, _ = x.shape
    depth = w_in.shape[0]
    t = batch * seq
    slopes = jnp.exp2(-8.0 * (jnp.arange(C_HEADS, dtype=F32) + 1.0) / C_HEADS)
    x2 = x.reshape(t, D_MODEL)
    for l in range(depth):
        w_main = jnp.concatenate([w_in[l, :, :OFF_BETA], w_in[l, :, OFF_C:]], axis=1).astype(BF16)
        w_small = jnp.pad(w_in[l, :, OFF_BETA:OFF_C].T, ((0, SMALL_ROWS - 2 * B_HEADS), (0, 0))).astype(BF16)
        conv_w3 = conv_w[l].reshape(B_CONV, 3 * B_HEADS, HEAD_DIM).transpose(1, 0, 2)

        proj, small = _in_proj(x2, norm1_g[l].reshape(1, D_MODEL), w_main, w_small)
        small3 = small.reshape(SMALL_ROWS, t // DN_CHUNK, DN_CHUNK)
        ya = _mixer_a(proj, sgu_norm_g[l], w_spatial[l], b_spatial[l])
        yb = _mixer_b(proj, small3, conv_w3, a_log[l], dt_bias[l], o_norm_g[l].reshape(1, HEAD_DIM),
                      batch=batch, seq=seq)
        yc = _mixer_c(proj, slopes, q_norm_g[l].reshape(1, HEAD_DIM), k_norm_g[l].reshape(1, HEAD_DIM),
                      batch=batch, seq=seq)
        x2, h2 = _out_proj(x2, ya, yb, yc, w_out[l].astype(BF16), norm2_g[l].reshape(1, D_MODEL))
        x2 = _ffn(h2, x2, w_gate_up[l].astype(BF16), w_down[l].astype(BF16))
    return x2.reshape(batch, seq, D_MODEL)
```

```python
import functools

import jax
import jax.numpy as jnp
from jax import lax
from jax.experimental import pallas as pl
from jax.experimental.pallas import tpu as pltpu

F32 = jnp.float32
BF16 = jnp.bfloat16

D_MODEL = 2048
HEAD_DIM = 128
A_GROUPS = 4
A_WIDTH = A_GROUPS * HEAD_DIM
A_CHUNK = 128
B_HEADS = 6
B_WIDTH = B_HEADS * HEAD_DIM
B_CONV = 4
C_HEADS = 6
C_WIDTH = C_HEADS * HEAD_DIM
C_BRANCHES = ((128, 1), (512, 4), (2048, 16))
C_BLOCK = 128
FFN_HIDDEN = 5632
EPS = 1e-6

OFF_B = 2 * A_WIDTH
OFF_BETA = OFF_B + 4 * B_WIDTH
OFF_C = OFF_BETA + 2 * B_HEADS
IN_TOTAL = OFF_C + 3 * C_WIDTH
SLAB_B_Q, SLAB_B_K, SLAB_B_V, SLAB_B_G = 0, 6, 12, 18
SLAB_C_Q, SLAB_C_K, SLAB_C_V = 24, 30, 36
SLAB_A_U, SLAB_A_V = 42, 46
N_SLABS = 50
SMALL_ROWS = 16

LANES = 128
DN_CHUNK = 128
DN_STEP = 512
DN_GROUP = 2
C_GROUP = 4
VMEM_LIMIT = 56 * 1024 * 1024

NT_DIMS = (((1,), (1,)), ((), ()))


def _mm(a, b):
    return jnp.dot(a.astype(BF16), b.astype(BF16), preferred_element_type=F32)


def _mm_nt(a, b):
    return lax.dot_general(a.astype(BF16), b.astype(BF16), NT_DIMS, preferred_element_type=F32)


def _sigmoid(x):
    return 1.0 / (1.0 + jnp.exp(-x))


def _silu(x):
    return x * _sigmoid(x)


def _gelu_tanh(x):
    return 0.5 * x * (1.0 + jnp.tanh(0.7978845608028654 * (x + 0.044715 * (x * x * x))))


def _rms(x, gain):
    return x * lax.rsqrt(jnp.mean(x * x, axis=-1, keepdims=True) + EPS) * gain


def _params(*sem):
    return pltpu.CompilerParams(dimension_semantics=sem, vmem_limit_bytes=VMEM_LIMIT)


def _in_proj_kernel(x_ref, g_ref, w_ref, ws_ref, o_ref, os_ref, h_scr, *, slabs_per_step):
    @pl.when(pl.program_id(1) == 0)
    def _():
        h = _rms(x_ref[...], g_ref[...]).astype(BF16)
        h_scr[...] = h
        small = lax.dot_general(ws_ref[...], h, NT_DIMS, preferred_element_type=F32)
        for c in range(small.shape[1] // DN_CHUNK):
            os_ref[c] = small[:, c * DN_CHUNK:(c + 1) * DN_CHUNK]

    res = jnp.dot(h_scr[...], w_ref[...], preferred_element_type=F32)
    for c in range(slabs_per_step):
        o_ref[c] = res[:, c * LANES:(c + 1) * LANES].astype(BF16)


def _in_proj(x2, gain, w, ws, *, tm=512, slabs_per_step=10):
    t = x2.shape[0]
    tn = slabs_per_step * LANES
    return pl.pallas_call(
        functools.partial(_in_proj_kernel, slabs_per_step=slabs_per_step),
        grid=(t // tm, N_SLABS // slabs_per_step),
        in_specs=[
            pl.BlockSpec((tm, D_MODEL), lambda i, j: (i, 0)),
            pl.BlockSpec((1, D_MODEL), lambda i, j: (0, 0)),
            pl.BlockSpec((D_MODEL, tn), lambda i, j: (0, j)),
            pl.BlockSpec((SMALL_ROWS, D_MODEL), lambda i, j: (0, 0)),
        ],
        out_specs=[
            pl.BlockSpec((slabs_per_step, tm, LANES), lambda i, j: (j, i, 0)),
            pl.BlockSpec((tm // DN_CHUNK, SMALL_ROWS, DN_CHUNK), lambda i, j: (i, 0, 0)),
        ],
        out_shape=[
            jax.ShapeDtypeStruct((N_SLABS, t, LANES), BF16),
            jax.ShapeDtypeStruct((t // DN_CHUNK, SMALL_ROWS, DN_CHUNK), F32),
        ],
        scratch_shapes=[pltpu.VMEM((tm, D_MODEL), BF16)],
        compiler_params=_params("parallel", "arbitrary"),
        name="in_proj",
    )(x2, gain, w, ws)


def _mixer_a_kernel(u01_ref, u23_ref, v01_ref, v23_ref, sg_ref, ws_ref, bs_ref, o_ref, *, chunks):
    ii = lax.broadcasted_iota(jnp.int32, (A_CHUNK, A_CHUNK), 0)
    jj = lax.broadcasted_iota(jnp.int32, (A_CHUNK, A_CHUNK), 1)
    for g in range(A_GROUPS):
        u_ref = (u01_ref, u23_ref)[g // 2]
        v_ref = (v01_ref, v23_ref)[g // 2]
        w = jnp.where(ii >= jj, ws_ref[g], 0.0).astype(BF16)
        gain = sg_ref[g:g + 1, :]
        bias = bs_ref[g]
        for c in range(chunks):
            rows = slice(c * A_CHUNK, (c + 1) * A_CHUNK)
            u = _gelu_tanh(u_ref[g % 2, rows, :].astype(F32))
            v = _rms(_gelu_tanh(v_ref[g % 2, rows, :].astype(F32)), gain)
            z = jnp.dot(w, v.astype(BF16), preferred_element_type=F32) + bias
            o_ref[g, rows, :] = (u * z).astype(BF16)


def _mixer_a(proj, sgu_g, w_s, b_s, *, tm=512):
    t = proj.shape[1]
    pair = lambda s0: pl.BlockSpec((2, tm, LANES), lambda i: (s0 // 2, i, 0))
    return pl.pallas_call(
        functools.partial(_mixer_a_kernel, chunks=tm // A_CHUNK),
        grid=(t // tm,),
        in_specs=[
            pair(SLAB_A_U), pair(SLAB_A_U + 2), pair(SLAB_A_V), pair(SLAB_A_V + 2),
            pl.BlockSpec((A_GROUPS, HEAD_DIM), lambda i: (0, 0)),
            pl.BlockSpec((A_GROUPS, A_CHUNK, A_CHUNK), lambda i: (0, 0, 0)),
            pl.BlockSpec((A_GROUPS, A_CHUNK, 1), lambda i: (0, 0, 0)),
        ],
        out_specs=pl.BlockSpec((A_GROUPS, tm, LANES), lambda i: (0, i, 0)),
        out_shape=jax.ShapeDtypeStruct((A_GROUPS, t, LANES), BF16),
        compiler_params=_params("parallel"),
        name="mixer_a",
    )(proj, proj, proj, proj, sgu_g, w_s, b_s.reshape(A_GROUPS, A_CHUNK, 1))


def _unit_lower_inverse(a_list, ii, jj):
    def same_block(shift):
        return (ii >> shift) == (jj >> shift)

    a_pow = [jnp.where(same_block(4), a, 0.0) for a in a_list]
    eye = jnp.where(ii == jj, 1.0, 0.0)
    p = [eye - x for x in a_pow]
    for _ in range(3):
        a_pow = [_mm(x, x) for x in a_pow]
        p = [y + _mm(y, x) for y, x in zip(p, a_pow)]
    for shift in (4, 5, 6):
        off = same_block(shift + 1) & jnp.logical_not(same_block(shift))
        t = [_mm(jnp.where(off, a, 0.0), y) for a, y in zip(a_list, p)]
        p = [y - _mm(y, x) for y, x in zip(p, t)]
    return p


def _mixer_b_kernel(alog_ref, dtb_ref, q_ref, k_ref, v_ref, gate_ref, small_ref, cw_ref, og_ref, o_ref,
                    xpad, qn, kn, vn, gb_scr, el_scr, st_scr, u_scr, w_scr, at_scr, qg_scr, kdt_scr, *, rows_per_step):
    nchunks = rows_per_step // DN_CHUNK
    pad = 8
    sq = (DN_CHUNK, DN_CHUNK)

    @pl.when(pl.program_id(1) == 0)
    def _():
        xpad[:, 0:pad, :] = jnp.zeros((3 * B_HEADS, pad, HEAD_DIM), F32)
        st_scr[...] = jnp.zeros_like(st_scr)

    for kind, (src, dst) in enumerate(((q_ref, qn), (k_ref, kn), (v_ref, vn))):
        for h in range(B_HEADS):
            idx = kind * B_HEADS + h
            xpad[idx, pad:pad + rows_per_step, :] = src[h].astype(F32)
            cw = cw_ref[idx]
            for t in range(nchunks):
                base = pad + t * DN_CHUNK
                y = cw[B_CONV - 1:B_CONV, :] * xpad[idx, base:base + DN_CHUNK, :]
                for s in range(1, B_CONV):
                    y = y + cw[B_CONV - 1 - s:B_CONV - s, :] * xpad[idx, base - s:base - s + DN_CHUNK, :]
                y = _silu(y)
                if kind < 2:
                    y = y * lax.rsqrt(jnp.sum(y * y, axis=-1, keepdims=True) + EPS)
                if kind == 0:
                    y = y * (HEAD_DIM ** -0.5)
                dst[h, t * DN_CHUNK:(t + 1) * DN_CHUNK, :] = y
            xpad[idx, 0:pad, :] = xpad[idx, rows_per_step:rows_per_step + pad, :]

    small = small_ref[...].reshape(nchunks * SMALL_ROWS, DN_CHUNK)
    j = lax.broadcasted_iota(jnp.int32, small.shape, 0) % SMALL_ROWS
    lane = lax.broadcasted_iota(jnp.int32, small.shape, 1)
    dtb = jnp.zeros_like(small)
    alog = jnp.zeros_like(small)
    for h in range(B_HEADS):
        dtb = jnp.where(j == B_HEADS + h, dtb_ref[h], dtb)
        alog = jnp.where(j == B_HEADS + h, alog_ref[h], alog)
    xa = small + dtb
    g = -jnp.exp(alog) * (jnp.maximum(xa, 0.0) + jnp.log(1.0 + jnp.exp(-jnp.abs(xa))))
    shift = 1
    while shift < DN_CHUNK:
        g = g + jnp.where(lane >= shift, pltpu.roll(g, shift, 1), 0.0)
        shift *= 2
    gb_scr[...] = jnp.where(j < B_HEADS, _sigmoid(small), g)

    ii = lax.broadcasted_iota(jnp.int32, sq, 0)
    jj = lax.broadcasted_iota(jnp.int32, sq, 1)

    for c0 in range(0, nchunks, DN_GROUP):
        items = [(c, h) for c in range(c0, c0 + DN_GROUP) for h in range(B_HEADS)]
        rows = {c: slice(c * DN_CHUNK, (c + 1) * DN_CHUNK) for c, _ in items}
        g_j = [jnp.broadcast_to(gb_scr[c * SMALL_ROWS + B_HEADS + h:c * SMALL_ROWS + B_HEADS + h + 1, :], sq)
               for c, h in items]
        b_i = [jnp.broadcast_to(gb_scr[c * SMALL_ROWS + h:c * SMALL_ROWS + h + 1, :], sq).T
               for c, h in items]
        g_i = [x.T for x in g_j]
        decay = [jnp.exp(jnp.where(ii >= jj, x - y, -jnp.inf)) for x, y in zip(g_i, g_j)]
        kb = [kn[h, rows[c], :] * b for (c, h), b in zip(items, b_i)]
        a = [jnp.where(ii > jj, _mm_nt(x, kn[h, rows[c], :]) * d, 0.0) for x, (c, h), d in zip(kb, items, decay)]
        t_inv = _unit_lower_inverse(a, ii, jj)
        e_i = [jnp.exp(x) for x in g_i]
        sol = [_mm(t, jnp.concatenate([vn[h, rows[c], :] * b, x * e], axis=1))
               for t, (c, h), b, x, e in zip(t_inv, items, b_i, kb, e_i)]
        attn = [_mm_nt(qn[h, rows[c], :], kn[h, rows[c], :]) * d for (c, h), d in zip(items, decay)]
        for i, (c, h) in enumerate(items):
            r = rows[c]
            g_last = jnp.broadcast_to(g_j[i][:, DN_CHUNK - 1:DN_CHUNK], sq)
            u_scr[h, r, :] = sol[i][:, :HEAD_DIM]
            w_scr[h, r, :] = sol[i][:, HEAD_DIM:].astype(BF16)
            at_scr[h, r, :] = attn[i].astype(BF16)
            qg_scr[h, r, :] = (qn[h, r, :] * e_i[i]).astype(BF16)
            kdt_scr[h, r, :] = (kn[h, r, :] * jnp.exp(g_last - g_i[i])).T.astype(BF16)
            el_scr[h, c:c + 1, :] = jnp.exp(g_last[0:1, :])

    heads = range(B_HEADS)
    state = [st_scr[h] for h in heads]
    for c in range(nchunks):
        r = slice(c * DN_CHUNK, (c + 1) * DN_CHUNK)
        sb = [s.astype(BF16) for s in state]
        v_new = [u_scr[h, r, :] - jnp.dot(w_scr[h, r, :], sb[h], preferred_element_type=F32) for h in heads]
        vb = [x.astype(BF16) for x in v_new]
        o = [jnp.dot(qg_scr[h, r, :], sb[h], preferred_element_type=F32)
             + jnp.dot(at_scr[h, r, :], vb[h], preferred_element_type=F32) for h in heads]
        state = [state[h] * el_scr[h, c:c + 1, :] + jnp.dot(kdt_scr[h, r, :], vb[h], preferred_element_type=F32)
                 for h in heads]
        for h in heads:
            y = _rms(o[h], og_ref[...]) * _silu(gate_ref[h, r, :].astype(F32))
            o_ref[h, r, :] = y.astype(BF16)
    for h in heads:
        st_scr[h] = state[h]


def _mixer_b(proj, small3, conv_w3, a_log, dt_bias, o_norm_g, *, batch, seq):
    t = proj.shape[1]
    steps = seq // DN_STEP
    nch = DN_STEP // DN_CHUNK
    slab = lambda s0: pl.BlockSpec((B_HEADS, DN_STEP, LANES), lambda b, s: (s0 // B_HEADS, b * steps + s, 0))
    smem = pl.BlockSpec(memory_space=pltpu.SMEM)
    per_head = lambda rows, dtype: pltpu.VMEM((B_HEADS, rows, HEAD_DIM), dtype)
    return pl.pallas_call(
        functools.partial(_mixer_b_kernel, rows_per_step=DN_STEP),
        grid=(batch, steps),
        in_specs=[
            smem, smem,
            slab(SLAB_B_Q), slab(SLAB_B_K), slab(SLAB_B_V), slab(SLAB_B_G),
            pl.BlockSpec((nch, SMALL_ROWS, DN_CHUNK), lambda b, s: (b * steps + s, 0, 0)),
            pl.BlockSpec((3 * B_HEADS, B_CONV, HEAD_DIM), lambda b, s: (0, 0, 0)),
            pl.BlockSpec((1, HEAD_DIM), lambda b, s: (0, 0)),
        ],
        out_specs=pl.BlockSpec((B_HEADS, DN_STEP, LANES), lambda b, s: (0, b * steps + s, 0)),
        out_shape=jax.ShapeDtypeStruct((B_HEADS, t, LANES), BF16),
        scratch_shapes=[
            pltpu.VMEM((3 * B_HEADS, DN_STEP + 8, HEAD_DIM), F32),
            per_head(DN_STEP, F32),
            per_head(DN_STEP, F32),
            per_head(DN_STEP, F32),
            pltpu.VMEM((nch * SMALL_ROWS, DN_CHUNK), F32),
            pltpu.VMEM((B_HEADS, nch, LANES), F32),
            per_head(HEAD_DIM, F32),
            per_head(DN_STEP, F32),
            per_head(DN_STEP, BF16),
            per_head(DN_STEP, BF16),
            per_head(DN_STEP, BF16),
            per_head(DN_STEP, BF16),
        ],
        compiler_params=_params("parallel", "arbitrary"),
        name="mixer_b",
    )(a_log, dt_bias, proj, proj, proj, proj, small3, conv_w3, o_norm_g)


def _mixer_c_kernel(slope_ref, q_ref, k_ref, v_ref, qg_ref, kg_ref, o_ref,
                    qs, ks, vs, ob, lb, *, seq):
    h = pl.program_id(1)
    slope = slope_ref[h]
    qs[...] = _rms(q_ref[...].astype(F32), qg_ref[...])
    ks[...] = _rms(k_ref[...].astype(F32), kg_ref[...])
    vs[...] = v_ref[...].astype(F32)

    qi = lax.broadcasted_iota(jnp.int32, (C_BLOCK, 2 * C_BLOCK), 0)
    kj = lax.broadcasted_iota(jnp.int32, (C_BLOCK, 2 * C_BLOCK), 1)
    delta = C_BLOCK + qi - kj
    scale = HEAD_DIM ** -0.5

    for br, (window, dil) in enumerate(C_BRANCHES):
        span = window // dil
        seg = seq // dil
        in_band = (delta >= 0) & (delta <= span)
        bias = jnp.where(in_band, -slope * (delta * dil).astype(F32), -jnp.inf)
        bias_cur = bias[:, C_BLOCK:]

        def rows(start, size, dil=dil):
            return pl.ds(start, size, stride=dil) if dil > 1 else pl.ds(start, size)

        def attend(items, br=br):
            s = [_mm_nt(qs[q, :], ks[k, :]) * scale + b for q, k, b in items]
            mx = [jnp.max(x, axis=-1, keepdims=True) for x in s]
            p = [jnp.exp(x - m).astype(BF16) for x, m in zip(s, mx)]
            acc = [jnp.dot(x, jnp.concatenate([vs[k, :].astype(BF16), jnp.ones((x.shape[1], LANES), BF16)], axis=1),
                           preferred_element_type=F32) for x, (_, k, _) in zip(p, items)]
            for (q, _, _), y, m in zip(items, acc, mx):
                den = y[:, HEAD_DIM:]
                ob[br, q, :] = y[:, :HEAD_DIM] / den
                lb[br, q, :] = m + jnp.log(den)

        nb = seg // C_BLOCK
        blocks = []
        for r in range(dil):
            blocks.append((rows(r, C_BLOCK), rows(r, C_BLOCK), bias_cur))
            for n in range(1, nb):
                blocks.append((rows(r + n * C_BLOCK * dil, C_BLOCK),
                               rows(r + (n - 1) * C_BLOCK * dil, 2 * C_BLOCK), bias))
        for g0 in range(0, len(blocks), C_GROUP):
            attend(blocks[g0:g0 + C_GROUP])

    def merge(n, carry):
        r = pl.ds(pl.multiple_of(n * C_BLOCK, C_BLOCK), C_BLOCK)
        l0, l1, l2 = lb[0, r, :], lb[1, r, :], lb[2, r, :]
        mx = jnp.maximum(jnp.maximum(l0, l1), l2)
        w0, w1, w2 = jnp.exp(l0 - mx), jnp.exp(l1 - mx), jnp.exp(l2 - mx)
        out = (w0 * ob[0, r, :] + w1 * ob[1, r, :] + w2 * ob[2, r, :]) / (w0 + w1 + w2)
        o_ref[r, :] = out.astype(BF16)
        return carry

    lax.fori_loop(0, seq // C_BLOCK, merge, 0)


def _mixer_c(proj, slopes, q_g, k_g, *, batch, seq):
    t = proj.shape[1]
    slab = lambda s0: pl.BlockSpec((None, seq, LANES), lambda b, h: (s0 + h, b, 0))
    vec = pl.BlockSpec((1, HEAD_DIM), lambda b, h: (0, 0))
    return pl.pallas_call(
        functools.partial(_mixer_c_kernel, seq=seq),
        grid=(batch, C_HEADS),
        in_specs=[pl.BlockSpec(memory_space=pltpu.SMEM),
                  slab(SLAB_C_Q), slab(SLAB_C_K), slab(SLAB_C_V), vec, vec],
        out_specs=pl.BlockSpec((None, seq, LANES), lambda b, h: (h, b, 0)),
        out_shape=jax.ShapeDtypeStruct((C_HEADS, t, LANES), BF16),
        scratch_shapes=[
            pltpu.VMEM((seq, HEAD_DIM), F32),
            pltpu.VMEM((seq, HEAD_DIM), F32),
            pltpu.VMEM((seq, HEAD_DIM), F32),
            pltpu.VMEM((len(C_BRANCHES), seq, HEAD_DIM), F32),
            pltpu.VMEM((len(C_BRANCHES), seq, LANES), F32),
        ],
        compiler_params=_params("parallel", "arbitrary"),
        name="mixer_c",
    )(slopes, proj, proj, proj, q_g, k_g)


def _out_proj_kernel(x_ref, ya_ref, yb_ref, yc_ref, w_ref, g_ref, xo_ref, ho_ref):
    mix = jnp.concatenate([ya_ref[i] for i in range(A_GROUPS)]
                          + [yb_ref[i] for i in range(B_HEADS)]
                          + [yc_ref[i] for i in range(C_HEADS)], axis=1)
    xn = x_ref[...] + jnp.dot(mix, w_ref[...], preferred_element_type=F32)
    xo_ref[...] = xn
    ho_ref[...] = _rms(xn, g_ref[...]).astype(BF16)


def _out_proj(x2, ya, yb, yc, w, gain, *, tm=512):
    t = x2.shape[0]
    return pl.pallas_call(
        _out_proj_kernel,
        grid=(t // tm,),
        in_specs=[
            pl.BlockSpec((tm, D_MODEL), lambda i: (i, 0)),
            pl.BlockSpec((A_GROUPS, tm, LANES), lambda i: (0, i, 0)),
            pl.BlockSpec((B_HEADS, tm, LANES), lambda i: (0, i, 0)),
            pl.BlockSpec((C_HEADS, tm, LANES), lambda i: (0, i, 0)),
            pl.BlockSpec((D_MODEL, D_MODEL), lambda i: (0, 0)),
            pl.BlockSpec((1, D_MODEL), lambda i: (0, 0)),
        ],
        out_specs=[
            pl.BlockSpec((tm, D_MODEL), lambda i: (i, 0)),
            pl.BlockSpec((tm, D_MODEL), lambda i: (i, 0)),
        ],
        out_shape=[
            jax.ShapeDtypeStruct((t, D_MODEL), F32),
            jax.ShapeDtypeStruct((t, D_MODEL), BF16),
        ],
        compiler_params=_params("parallel"),
        name="out_proj",
    )(x2, ya, yb, yc, w, gain)


def _ffn_kernel(h_ref, x_ref, wg_ref, wu_ref, wd_ref, o_ref):
    @pl.when(pl.program_id(1) == 0)
    def _():
        o_ref[...] = x_ref[...]

    h = h_ref[...]
    gt = jnp.dot(h, wg_ref[...], preferred_element_type=F32)
    up = jnp.dot(h, wu_ref[...], preferred_element_type=F32)
    act = (_silu(gt) * up).astype(BF16)
    o_ref[...] += jnp.dot(act, wd_ref[...], preferred_element_type=F32)


def _ffn(h2, x2, w_gate_up, w_down, *, tm=512, tf=512):
    t = x2.shape[0]
    nf = FFN_HIDDEN // tf
    return pl.pallas_call(
        _ffn_kernel,
        grid=(t // tm, nf),
        in_specs=[
            pl.BlockSpec((tm, D_MODEL), lambda i, j: (i, 0)),
            pl.BlockSpec((tm, D_MODEL), lambda i, j: (i, 0)),
            pl.BlockSpec((D_MODEL, tf), lambda i, j: (0, j)),
            pl.BlockSpec((D_MODEL, tf), lambda i, j: (0, nf + j)),
            pl.BlockSpec((tf, D_MODEL), lambda i, j: (j, 0)),
        ],
        out_specs=pl.BlockSpec((tm, D_MODEL), lambda i, j: (i, 0)),
        out_shape=jax.ShapeDtypeStruct((t, D_MODEL), F32),
        compiler_params=_params("parallel", "arbitrary"),
        name="ffn",
    )(h2, x2, w_gate_up, w_gate_up, w_down)


def kernel(x, norm1_g, w_in, sgu_norm_g, w_spatial, b_spatial, conv_w, a_log, dt_bias, o_norm_g,
           q_norm_g, k_norm_g, w_out, norm2_g, w_gate_up, w_down):
    batch, seq, _ = x.shape
    depth = w_in.shape[0]
    t = batch * seq
    slopes = jnp.exp2(-8.0 * (jnp.arange(C_HEADS, dtype=F32) + 1.0) / C_HEADS)
    x2 = x.reshape(t, D_MODEL)
    for l in range(depth):
        w_main = jnp.concatenate([w_in[l, :, OFF_B:OFF_BETA], w_in[l, :, OFF_C:], w_in[l, :, :OFF_B]],
                                 axis=1).astype(BF16)
        w_small = jnp.pad(w_in[l, :, OFF_BETA:OFF_C].T, ((0, SMALL_ROWS - 2 * B_HEADS), (0, 0))).astype(BF16)
        conv_w3 = conv_w[l].reshape(B_CONV, 3 * B_HEADS, HEAD_DIM).transpose(1, 0, 2)

        proj, small3 = _in_proj(x2, norm1_g[l].reshape(1, D_MODEL), w_main, w_small)
        ya = _mixer_a(proj, sgu_norm_g[l], w_spatial[l], b_spatial[l])
        yb = _mixer_b(proj, small3, conv_w3, a_log[l], dt_bias[l], o_norm_g[l].reshape(1, HEAD_DIM),
                      batch=batch, seq=seq)
        yc = _mixer_c(proj, slopes, q_norm_g[l].reshape(1, HEAD_DIM), k_norm_g[l].reshape(1, HEAD_DIM),
                      batch=batch, seq=seq)
        x2, h2 = _out_proj(x2, ya, yb, yc, w_out[l].astype(BF16), norm2_g[l].reshape(1, D_MODEL))
        x2 = _ffn(h2, x2, w_gate_up[l].astype(BF16), w_down[l].astype(BF16))
    return x2.reshape(batch, seq, D_MODEL)
```

```python
import functools

import jax
import jax.numpy as jnp
from jax import lax
from jax.experimental import pallas as pl
from jax.experimental.pallas import tpu as pltpu

F32 = jnp.float32
BF16 = jnp.bfloat16

D_MODEL = 2048
HEAD_DIM = 128
A_GROUPS = 4
A_WIDTH = A_GROUPS * HEAD_DIM
A_CHUNK = 128
B_HEADS = 6
B_WIDTH = B_HEADS * HEAD_DIM
B_CONV = 4
C_HEADS = 6
C_WIDTH = C_HEADS * HEAD_DIM
C_BRANCHES = ((128, 1), (512, 4), (2048, 16))
C_BLOCK = 128
FFN_HIDDEN = 5632
EPS = 1e-6

OFF_B = 2 * A_WIDTH
OFF_BETA = OFF_B + 4 * B_WIDTH
OFF_C = OFF_BETA + 2 * B_HEADS
IN_TOTAL = OFF_C + 3 * C_WIDTH
SLABS_A, SLABS_B, SLABS_C = 2 * A_GROUPS, 4 * B_HEADS, 3 * C_HEADS
SMALL_ROWS = 16
IN_STEPS_AB = (SLABS_A + SLABS_B) // SLABS_A
IN_STEPS_C = 3

LANES = 128
DN_CHUNK = 128
DN_STEP = 512
DN_GROUP = 2
C_GROUP = 4
VMEM_LIMIT = 56 * 1024 * 1024

NT_DIMS = (((1,), (1,)), ((), ()))


def _mm(a, b):
    return jnp.dot(a.astype(BF16), b.astype(BF16), preferred_element_type=F32)


def _mm_nt(a, b):
    return lax.dot_general(a.astype(BF16), b.astype(BF16), NT_DIMS, preferred_element_type=F32)


def _sigmoid(x):
    return 1.0 / (1.0 + jnp.exp(-x))


def _silu(x):
    return x * _sigmoid(x)


def _gelu_tanh(x):
    return 0.5 * x * (1.0 + jnp.tanh(0.7978845608028654 * (x + 0.044715 * (x * x * x))))


def _rms(x, gain):
    return x * lax.rsqrt(jnp.mean(x * x, axis=-1, keepdims=True) + EPS) * gain


def _params(*sem):
    return pltpu.CompilerParams(dimension_semantics=sem, vmem_limit_bytes=VMEM_LIMIT)


def _in_proj_kernel(x_ref, g_ref, wab_ref, wc_ref, ws_ref, oa_ref, ob_ref, oc_ref, os_ref, h_scr):
    j = pl.program_id(1)

    @pl.when(j == 0)
    def _():
        h = _rms(x_ref[...], g_ref[...]).astype(BF16)
        h_scr[...] = h
        small = lax.dot_general(ws_ref[...], h, NT_DIMS, preferred_element_type=F32)
        for c in range(small.shape[1] // DN_CHUNK):
            os_ref[c] = small[:, c * DN_CHUNK:(c + 1) * DN_CHUNK]

    def project(w_ref, o_ref):
        res = jnp.dot(h_scr[...], w_ref[...], preferred_element_type=F32)
        for c in range(o_ref.shape[0]):
            o_ref[c] = res[:, c * LANES:(c + 1) * LANES].astype(BF16)

    pl.when(j == 0)(lambda: project(wab_ref, oa_ref))
    pl.when((j >= 1) & (j < IN_STEPS_AB))(lambda: project(wab_ref, ob_ref))
    pl.when(j >= IN_STEPS_AB)(lambda: project(wc_ref, oc_ref))


def _in_proj(x2, gain, w_ab, w_c, w_small, layer, *, tm=1024):
    t = x2.shape[0]
    tn_ab = SLABS_A * LANES
    tn_c = (SLABS_C // IN_STEPS_C) * LANES
    clamp = lambda v, hi: jnp.minimum(jnp.maximum(v, 0), hi)
    return pl.pallas_call(
        _in_proj_kernel,
        grid=(t // tm, IN_STEPS_AB + IN_STEPS_C),
        in_specs=[
            pl.BlockSpec((tm, D_MODEL), lambda i, j: (i, 0)),
            pl.BlockSpec((None, 1, D_MODEL), lambda i, j: (layer, 0, 0)),
            pl.BlockSpec((None, D_MODEL, tn_ab), lambda i, j: (layer, 0, clamp(j, IN_STEPS_AB - 1))),
            pl.BlockSpec((None, D_MODEL, tn_c), lambda i, j: (layer, 0, clamp(j - IN_STEPS_AB, IN_STEPS_C - 1))),
            pl.BlockSpec((None, SMALL_ROWS, D_MODEL), lambda i, j: (layer, 0, 0)),
        ],
        out_specs=[
            pl.BlockSpec((SLABS_A, tm, LANES), lambda i, j: (0, i, 0)),
            pl.BlockSpec((SLABS_A, tm, LANES), lambda i, j: (clamp(j - 1, IN_STEPS_AB - 2), i, 0)),
            pl.BlockSpec((SLABS_C // IN_STEPS_C, tm, LANES), lambda i, j: (clamp(j - IN_STEPS_AB, IN_STEPS_C - 1), i, 0)),
            pl.BlockSpec((tm // DN_CHUNK, SMALL_ROWS, DN_CHUNK), lambda i, j: (i, 0, 0)),
        ],
        out_shape=[
            jax.ShapeDtypeStruct((SLABS_A, t, LANES), BF16),
            jax.ShapeDtypeStruct((SLABS_B, t, LANES), BF16),
            jax.ShapeDtypeStruct((SLABS_C, t, LANES), BF16),
            jax.ShapeDtypeStruct((t // DN_CHUNK, SMALL_ROWS, DN_CHUNK), F32),
        ],
        scratch_shapes=[pltpu.VMEM((tm, D_MODEL), BF16)],
        compiler_params=_params("parallel", "arbitrary"),
        name="in_proj",
    )(x2, gain, w_ab, w_c, w_small)


def _mixer_a_kernel(p_ref, sg_ref, ws_ref, bs_ref, o_ref, *, chunks):
    ii = lax.broadcasted_iota(jnp.int32, (A_CHUNK, A_CHUNK), 0)
    jj = lax.broadcasted_iota(jnp.int32, (A_CHUNK, A_CHUNK), 1)
    for g in range(A_GROUPS):
        w = jnp.where(ii >= jj, ws_ref[g], 0.0).astype(BF16)
        gain = sg_ref[g:g + 1, :]
        bias = bs_ref[g]
        for c in range(chunks):
            rows = slice(c * A_CHUNK, (c + 1) * A_CHUNK)
            u = _gelu_tanh(p_ref[g, rows, :].astype(F32))
            v = _rms(_gelu_tanh(p_ref[A_GROUPS + g, rows, :].astype(F32)), gain)
            z = jnp.dot(w, v.astype(BF16), preferred_element_type=F32) + bias
            o_ref[g, rows, :] = (u * z).astype(BF16)


def _mixer_a(proj, sgu_g, w_s, b_s, *, tm=512):
    t = proj.shape[1]
    return pl.pallas_call(
        functools.partial(_mixer_a_kernel, chunks=tm // A_CHUNK),
        grid=(t // tm,),
        in_specs=[
            pl.BlockSpec((SLABS_A, tm, LANES), lambda i: (0, i, 0)),
            pl.BlockSpec((A_GROUPS, HEAD_DIM), lambda i: (0, 0)),
            pl.BlockSpec((A_GROUPS, A_CHUNK, A_CHUNK), lambda i: (0, 0, 0)),
            pl.BlockSpec((A_GROUPS, A_CHUNK, 1), lambda i: (0, 0, 0)),
        ],
        out_specs=pl.BlockSpec((A_GROUPS, tm, LANES), lambda i: (0, i, 0)),
        out_shape=jax.ShapeDtypeStruct((A_GROUPS, t, LANES), BF16),
        compiler_params=_params("parallel"),
        name="mixer_a",
    )(proj, sgu_g, w_s, b_s.reshape(A_GROUPS, A_CHUNK, 1))


def _unit_lower_inverse(a_list, ii, jj):
    def same_block(shift):
        return (ii >> shift) == (jj >> shift)

    a_pow = [jnp.where(same_block(4), a, 0.0) for a in a_list]
    eye = jnp.where(ii == jj, 1.0, 0.0)
    p = [eye - x for x in a_pow]
    for _ in range(3):
        a_pow = [_mm(x, x) for x in a_pow]
        p = [y + _mm(y, x) for y, x in zip(p, a_pow)]
    for shift in (4, 5, 6):
        off = same_block(shift + 1) & jnp.logical_not(same_block(shift))
        t = [_mm(jnp.where(off, a, 0.0), y) for a, y in zip(a_list, p)]
        p = [y - _mm(y, x) for y, x in zip(p, t)]
    return p


def _mixer_b_kernel(alog_ref, dtb_ref, q_ref, k_ref, v_ref, gate_ref, small_ref, cw_ref, og_ref, o_ref,
                    xpad, qn, kn, vn, gb_scr, el_scr, st_scr, u_scr, w_scr, at_scr, qg_scr, kdt_scr, *, rows_per_step):
    nchunks = rows_per_step // DN_CHUNK
    pad = 8
    sq = (DN_CHUNK, DN_CHUNK)

    @pl.when(pl.program_id(1) == 0)
    def _():
        xpad[:, 0:pad, :] = jnp.zeros((3 * B_HEADS, pad, HEAD_DIM), F32)
        st_scr[...] = jnp.zeros_like(st_scr)

    for kind, (src, dst) in enumerate(((q_ref, qn), (k_ref, kn), (v_ref, vn))):
        for h in range(B_HEADS):
            idx = kind * B_HEADS + h
            xpad[idx, pad:pad + rows_per_step, :] = src[h].astype(F32)
            cw = cw_ref[idx]
            for t in range(nchunks):
                base = pad + t * DN_CHUNK
                y = cw[B_CONV - 1:B_CONV, :] * xpad[idx, base:base + DN_CHUNK, :]
                for s in range(1, B_CONV):
                    y = y + cw[B_CONV - 1 - s:B_CONV - s, :] * xpad[idx, base - s:base - s + DN_CHUNK, :]
                y = _silu(y)
                if kind < 2:
                    y = y * lax.rsqrt(jnp.sum(y * y, axis=-1, keepdims=True) + EPS)
                if kind == 0:
                    y = y * (HEAD_DIM ** -0.5)
                dst[h, t * DN_CHUNK:(t + 1) * DN_CHUNK, :] = y
            xpad[idx, 0:pad, :] = xpad[idx, rows_per_step:rows_per_step + pad, :]

    small = small_ref[...].reshape(nchunks * SMALL_ROWS, DN_CHUNK)
    j = lax.broadcasted_iota(jnp.int32, small.shape, 0) % SMALL_ROWS
    lane = lax.broadcasted_iota(jnp.int32, small.shape, 1)
    dtb = jnp.zeros_like(small)
    alog = jnp.zeros_like(small)
    for h in range(B_HEADS):
        dtb = jnp.where(j == B_HEADS + h, dtb_ref[h], dtb)
        alog = jnp.where(j == B_HEADS + h, alog_ref[h], alog)
    xa = small + dtb
    g = -jnp.exp(alog) * (jnp.maximum(xa, 0.0) + jnp.log(1.0 + jnp.exp(-jnp.abs(xa))))
    shift = 1
    while shift < DN_CHUNK:
        g = g + jnp.where(lane >= shift, pltpu.roll(g, shift, 1), 0.0)
        shift *= 2
    gb_scr[...] = jnp.where(j < B_HEADS, _sigmoid(small), g)

    ii = lax.broadcasted_iota(jnp.int32, sq, 0)
    jj = lax.broadcasted_iota(jnp.int32, sq, 1)

    for c0 in range(0, nchunks, DN_GROUP):
        items = [(c, h) for c in range(c0, c0 + DN_GROUP) for h in range(B_HEADS)]
        rows = {c: slice(c * DN_CHUNK, (c + 1) * DN_CHUNK) for c, _ in items}
        g_j = [jnp.broadcast_to(gb_scr[c * SMALL_ROWS + B_HEADS + h:c * SMALL_ROWS + B_HEADS + h + 1, :], sq)
               for c, h in items]
        b_i = [jnp.broadcast_to(gb_scr[c * SMALL_ROWS + h:c * SMALL_ROWS + h + 1, :], sq).T
               for c, h in items]
        g_i = [x.T for x in g_j]
        decay = [jnp.exp(jnp.where(ii >= jj, x - y, -jnp.inf)) for x, y in zip(g_i, g_j)]
        kb = [kn[h, rows[c], :] * b for (c, h), b in zip(items, b_i)]
        a = [jnp.where(ii > jj, _mm_nt(x, kn[h, rows[c], :]) * d, 0.0) for x, (c, h), d in zip(kb, items, decay)]
        t_inv = _unit_lower_inverse(a, ii, jj)
        e_i = [jnp.exp(x) for x in g_i]
        sol = [_mm(t, jnp.concatenate([vn[h, rows[c], :] * b, x * e], axis=1))
               for t, (c, h), b, x, e in zip(t_inv, items, b_i, kb, e_i)]
        attn = [_mm_nt(qn[h, rows[c], :], kn[h, rows[c], :]) * d for (c, h), d in zip(items, decay)]
        for i, (c, h) in enumerate(items):
            r = rows[c]
            g_last = jnp.broadcast_to(g_j[i][:, DN_CHUNK - 1:DN_CHUNK], sq)
            u_scr[h, r, :] = sol[i][:, :HEAD_DIM]
            w_scr[h, r, :] = sol[i][:, HEAD_DIM:].astype(BF16)
            at_scr[h, r, :] = attn[i].astype(BF16)
            qg_scr[h, r, :] = (qn[h, r, :] * e_i[i]).astype(BF16)
            kdt_scr[h, r, :] = (kn[h, r, :] * jnp.exp(g_last - g_i[i])).T.astype(BF16)
            el_scr[h, c:c + 1, :] = jnp.exp(g_last[0:1, :])

    heads = range(B_HEADS)
    state = [st_scr[h] for h in heads]
    for c in range(nchunks):
        r = slice(c * DN_CHUNK, (c + 1) * DN_CHUNK)
        sb = [s.astype(BF16) for s in state]
        v_new = [u_scr[h, r, :] - jnp.dot(w_scr[h, r, :], sb[h], preferred_element_type=F32) for h in heads]
        vb = [x.astype(BF16) for x in v_new]
        o = [jnp.dot(qg_scr[h, r, :], sb[h], preferred_element_type=F32)
             + jnp.dot(at_scr[h, r, :], vb[h], preferred_element_type=F32) for h in heads]
        state = [state[h] * el_scr[h, c:c + 1, :] + jnp.dot(kdt_scr[h, r, :], vb[h], preferred_element_type=F32)
                 for h in heads]
        for h in heads:
            y = _rms(o[h], og_ref[...]) * _silu(gate_ref[h, r, :].astype(F32))
            o_ref[h, r, :] = y.astype(BF16)
    for h in heads:
        st_scr[h] = state[h]


def _mixer_b(proj, small3, conv_w3, a_log, dt_bias, o_norm_g, *, batch, seq):
    t = proj.shape[1]
    steps = seq // DN_STEP
    nch = DN_STEP // DN_CHUNK
    slab = lambda n: pl.BlockSpec((B_HEADS, DN_STEP, LANES), lambda b, s: (n, b * steps + s, 0))
    smem = pl.BlockSpec(memory_space=pltpu.SMEM)
    per_head = lambda rows, dtype: pltpu.VMEM((B_HEADS, rows, HEAD_DIM), dtype)
    return pl.pallas_call(
        functools.partial(_mixer_b_kernel, rows_per_step=DN_STEP),
        grid=(batch, steps),
        in_specs=[
            smem, smem,
            slab(0), slab(1), slab(2), slab(3),
            pl.BlockSpec((nch, SMALL_ROWS, DN_CHUNK), lambda b, s: (b * steps + s, 0, 0)),
            pl.BlockSpec((3 * B_HEADS, B_CONV, HEAD_DIM), lambda b, s: (0, 0, 0)),
            pl.BlockSpec((1, HEAD_DIM), lambda b, s: (0, 0)),
        ],
        out_specs=pl.BlockSpec((B_HEADS, DN_STEP, LANES), lambda b, s: (0, b * steps + s, 0)),
        out_shape=jax.ShapeDtypeStruct((B_HEADS, t, LANES), BF16),
        scratch_shapes=[
            pltpu.VMEM((3 * B_HEADS, DN_STEP + 8, HEAD_DIM), F32),
            per_head(DN_STEP, F32),
            per_head(DN_STEP, F32),
            per_head(DN_STEP, F32),
            pltpu.VMEM((nch * SMALL_ROWS, DN_CHUNK), F32),
            pltpu.VMEM((B_HEADS, nch, LANES), F32),
            per_head(HEAD_DIM, F32),
            per_head(DN_STEP, F32),
            per_head(DN_STEP, BF16),
            per_head(DN_STEP, BF16),
            per_head(DN_STEP, BF16),
            per_head(DN_STEP, BF16),
        ],
        compiler_params=_params("parallel", "arbitrary"),
        name="mixer_b",
    )(a_log, dt_bias, proj, proj, proj, proj, small3, conv_w3, o_norm_g)


def _mixer_c_kernel(slope_ref, q_ref, k_ref, v_ref, qg_ref, kg_ref, o_ref,
                    qs, ks, vs, ob, lb, *, seq):
    h = pl.program_id(1)
    slope = slope_ref[h]
    qs[...] = _rms(q_ref[...].astype(F32), qg_ref[...])
    ks[...] = _rms(k_ref[...].astype(F32), kg_ref[...])
    vs[...] = v_ref[...].astype(F32)

    qi = lax.broadcasted_iota(jnp.int32, (C_BLOCK, 2 * C_BLOCK), 0)
    kj = lax.broadcasted_iota(jnp.int32, (C_BLOCK, 2 * C_BLOCK), 1)
    delta = C_BLOCK + qi - kj
    scale = HEAD_DIM ** -0.5

    for br, (window, dil) in enumerate(C_BRANCHES):
        span = window // dil
        seg = seq // dil
        in_band = (delta >= 0) & (delta <= span)
        bias = jnp.where(in_band, -slope * (delta * dil).astype(F32), -jnp.inf)
        bias_cur = bias[:, C_BLOCK:]

        def rows(start, size, dil=dil):
            return pl.ds(start, size, stride=dil) if dil > 1 else pl.ds(start, size)

        def attend(items, br=br):
            s = [_mm_nt(qs[q, :], ks[k, :]) * scale + b for q, k, b in items]
            mx = [jnp.max(x, axis=-1, keepdims=True) for x in s]
            p = [jnp.exp(x - m).astype(BF16) for x, m in zip(s, mx)]
            acc = [jnp.dot(x, jnp.concatenate([vs[k, :].astype(BF16), jnp.ones((x.shape[1], LANES), BF16)], axis=1),
                           preferred_element_type=F32) for x, (_, k, _) in zip(p, items)]
            for (q, _, _), y, m in zip(items, acc, mx):
                den = y[:, HEAD_DIM:]
                ob[br, q, :] = y[:, :HEAD_DIM] / den
                lb[br, q, :] = m + jnp.log(den)

        nb = seg // C_BLOCK
        blocks = []
        for r in range(dil):
            blocks.append((rows(r, C_BLOCK), rows(r, C_BLOCK), bias_cur))
            for n in range(1, nb):
                blocks.append((rows(r + n * C_BLOCK * dil, C_BLOCK),
                               rows(r + (n - 1) * C_BLOCK * dil, 2 * C_BLOCK), bias))
        for g0 in range(0, len(blocks), C_GROUP):
            attend(blocks[g0:g0 + C_GROUP])

    def merge(n, carry):
        r = pl.ds(pl.multiple_of(n * C_BLOCK, C_BLOCK), C_BLOCK)
        l0, l1, l2 = lb[0, r, :], lb[1, r, :], lb[2, r, :]
        mx = jnp.maximum(jnp.maximum(l0, l1), l2)
        w0, w1, w2 = jnp.exp(l0 - mx), jnp.exp(l1 - mx), jnp.exp(l2 - mx)
        out = (w0 * ob[0, r, :] + w1 * ob[1, r, :] + w2 * ob[2, r, :]) / (w0 + w1 + w2)
        o_ref[r, :] = out.astype(BF16)
        return carry

    lax.fori_loop(0, seq // C_BLOCK, merge, 0)


def _mixer_c(proj, slopes, q_g, k_g, *, batch, seq):
    t = proj.shape[1]
    slab = lambda n: pl.BlockSpec((None, seq, LANES), lambda b, h: (n * C_HEADS + h, b, 0))
    vec = pl.BlockSpec((1, HEAD_DIM), lambda b, h: (0, 0))
    return pl.pallas_call(
        functools.partial(_mixer_c_kernel, seq=seq),
        grid=(batch, C_HEADS),
        in_specs=[pl.BlockSpec(memory_space=pltpu.SMEM),
                  slab(0), slab(1), slab(2), vec, vec],
        out_specs=pl.BlockSpec((None, seq, LANES), lambda b, h: (h, b, 0)),
        out_shape=jax.ShapeDtypeStruct((C_HEADS, t, LANES), BF16),
        scratch_shapes=[
            pltpu.VMEM((seq, HEAD_DIM), F32),
            pltpu.VMEM((seq, HEAD_DIM), F32),
            pltpu.VMEM((seq, HEAD_DIM), F32),
            pltpu.VMEM((len(C_BRANCHES), seq, HEAD_DIM), F32),
            pltpu.VMEM((len(C_BRANCHES), seq, LANES), F32),
        ],
        compiler_params=_params("parallel", "arbitrary"),
        name="mixer_c",
    )(slopes, proj, proj, proj, q_g, k_g)


def _out_proj_kernel(x_ref, ya_ref, yb_ref, yc_ref, w_ref, g_ref, xo_ref, ho_ref):
    mix = jnp.concatenate([ya_ref[i] for i in range(A_GROUPS)]
                          + [yb_ref[i] for i in range(B_HEADS)]
                          + [yc_ref[i] for i in range(C_HEADS)], axis=1)
    xn = x_ref[...] + jnp.dot(mix, w_ref[...], preferred_element_type=F32)
    xo_ref[...] = xn
    ho_ref[...] = _rms(xn, g_ref[...]).astype(BF16)


def _out_proj(x2, ya, yb, yc, w, gain, layer, *, tm=512):
    t = x2.shape[0]
    return pl.pallas_call(
        _out_proj_kernel,
        grid=(t // tm,),
        in_specs=[
            pl.BlockSpec((tm, D_MODEL), lambda i: (i, 0)),
            pl.BlockSpec((A_GROUPS, tm, LANES), lambda i: (0, i, 0)),
            pl.BlockSpec((B_HEADS, tm, LANES), lambda i: (0, i, 0)),
            pl.BlockSpec((C_HEADS, tm, LANES), lambda i: (0, i, 0)),
            pl.BlockSpec((None, D_MODEL, D_MODEL), lambda i: (layer, 0, 0), pipeline_mode=pl.Buffered(1)),
            pl.BlockSpec((None, 1, D_MODEL), lambda i: (layer, 0, 0)),
        ],
        out_specs=[
            pl.BlockSpec((tm, D_MODEL), lambda i: (i, 0)),
            pl.BlockSpec((tm, D_MODEL), lambda i: (i, 0)),
        ],
        out_shape=[
            jax.ShapeDtypeStruct((t, D_MODEL), F32),
            jax.ShapeDtypeStruct((t, D_MODEL), BF16),
        ],
        compiler_params=_params("parallel"),
        name="out_proj",
    )(x2, ya, yb, yc, w, gain)


def _ffn_kernel(h_ref, x_ref, wg_ref, wu_ref, wd_ref, o_ref):
    @pl.when(pl.program_id(1) == 0)
    def _():
        o_ref[...] = x_ref[...]

    h = h_ref[...]
    gt = jnp.dot(h, wg_ref[...], preferred_element_type=F32)
    up = jnp.dot(h, wu_ref[...], preferred_element_type=F32)
    act = (_silu(gt) * up).astype(BF16)
    o_ref[...] += jnp.dot(act, wd_ref[...], preferred_element_type=F32)


def _ffn(h2, x2, w_gate_up, w_down, layer, *, tm=1024, tf=512):
    t = x2.shape[0]
    nf = FFN_HIDDEN // tf
    return pl.pallas_call(
        _ffn_kernel,
        grid=(t // tm, nf),
        in_specs=[
            pl.BlockSpec((tm, D_MODEL), lambda i, j: (i, 0)),
            pl.BlockSpec((tm, D_MODEL), lambda i, j: (i, 0), pipeline_mode=pl.Buffered(1)),
            pl.BlockSpec((None, D_MODEL, tf), lambda i, j: (layer, 0, j)),
            pl.BlockSpec((None, D_MODEL, tf), lambda i, j: (layer, 0, nf + j)),
            pl.BlockSpec((None, tf, D_MODEL), lambda i, j: (layer, j, 0)),
        ],
        out_specs=pl.BlockSpec((tm, D_MODEL), lambda i, j: (i, 0)),
        out_shape=jax.ShapeDtypeStruct((t, D_MODEL), F32),
        compiler_params=_params("parallel", "arbitrary"),
        name="ffn",
    )(h2, x2, w_gate_up, w_gate_up, w_down)


def kernel(x, norm1_g, w_in, sgu_norm_g, w_spatial, b_spatial, conv_w, a_log, dt_bias, o_norm_g,
           q_norm_g, k_norm_g, w_out, norm2_g, w_gate_up, w_down):
    batch, seq, _ = x.shape
    depth = w_in.shape[0]
    t = batch * seq
    slopes = jnp.exp2(-8.0 * (jnp.arange(C_HEADS, dtype=F32) + 1.0) / C_HEADS)
    w_ab = w_in[:, :, :OFF_BETA].astype(BF16)
    w_c = w_in[:, :, OFF_C:].astype(BF16)
    w_small = jnp.pad(jnp.swapaxes(w_in[:, :, OFF_BETA:OFF_C], 1, 2),
                      ((0, 0), (0, SMALL_ROWS - 2 * B_HEADS), (0, 0))).astype(BF16)
    w_out_b = w_out.astype(BF16)
    w_gate_up_b = w_gate_up.astype(BF16)
    w_down_b = w_down.astype(BF16)
    conv_w3 = conv_w.reshape(depth, B_CONV, 3 * B_HEADS, HEAD_DIM).transpose(0, 2, 1, 3)
    norm1 = norm1_g.reshape(depth, 1, D_MODEL)
    norm2 = norm2_g.reshape(depth, 1, D_MODEL)

    x2 = x.reshape(t, D_MODEL)
    for l in range(depth):
        pa, pb, pc, small3 = _in_proj(x2, norm1, w_ab, w_c, w_small, l)
        ya = _mixer_a(pa, sgu_norm_g[l], w_spatial[l], b_spatial[l])
        yb = _mixer_b(pb, small3, conv_w3[l], a_log[l], dt_bias[l], o_norm_g[l].reshape(1, HEAD_DIM),
                      batch=batch, seq=seq)
        yc = _mixer_c(pc, slopes, q_norm_g[l].reshape(1, HEAD_DIM), k_norm_g[l].reshape(1, HEAD_DIM),
                      batch=batch, seq=seq)
        x2, h2 = _out_proj(x2, ya, yb, yc, w_out_b, norm2, l)
        x2 = _ffn(h2, x2, w_gate_up_b, w_down_b, l)
    return x2.reshape(batch, seq, D_MODEL)
```

```python
import functools

import jax
import jax.numpy as jnp
from jax import lax
from jax.experimental import pallas as pl
from jax.experimental.pallas import tpu as pltpu

F32 = jnp.float32
BF16 = jnp.bfloat16

D_MODEL = 2048
HEAD_DIM = 128
A_GROUPS = 4
A_WIDTH = A_GROUPS * HEAD_DIM
A_CHUNK = 128
B_HEADS = 6
B_WIDTH = B_HEADS * HEAD_DIM
B_CONV = 4
C_HEADS = 6
C_WIDTH = C_HEADS * HEAD_DIM
C_BRANCHES = ((128, 1), (512, 4), (2048, 16))
C_BLOCK = 128
FFN_HIDDEN = 5632
EPS = 1e-6

OFF_B = 2 * A_WIDTH
OFF_BETA = OFF_B + 4 * B_WIDTH
OFF_C = OFF_BETA + 2 * B_HEADS
IN_TOTAL = OFF_C + 3 * C_WIDTH
SLABS_A, SLABS_B, SLABS_C = 2 * A_GROUPS, 4 * B_HEADS, 3 * C_HEADS
SMALL_ROWS = 16
IN_STEPS_AB = (SLABS_A + SLABS_B) // SLABS_A
IN_STEPS_C = 3

LANES = 128
DN_CHUNK = 128
DN_STEP = 512
DN_GROUP = 2
C_GROUP = 8
VMEM_LIMIT = 56 * 1024 * 1024

NT_DIMS = (((1,), (1,)), ((), ()))
NN_DIMS = (((1,), (0,)), ((), ()))


def _mm(a, b):
    return jnp.dot(a.astype(BF16), b.astype(BF16), preferred_element_type=F32)


def _mm_nt(a, b):
    return lax.dot_general(a.astype(BF16), b.astype(BF16), NT_DIMS, preferred_element_type=F32)


def _sigmoid(x):
    return 1.0 / (1.0 + jnp.exp(-x))


def _silu(x):
    return x * _sigmoid(x)


def _gelu_tanh(x):
    return 0.5 * x * (1.0 + jnp.tanh(0.7978845608028654 * (x + 0.044715 * (x * x * x))))


def _rms(x, gain):
    return x * lax.rsqrt(jnp.mean(x * x, axis=-1, keepdims=True) + EPS) * gain


def _params(*sem):
    return pltpu.CompilerParams(dimension_semantics=sem, vmem_limit_bytes=VMEM_LIMIT)


def _cast_kernel(w_ref, o_ref):
    o_ref[...] = w_ref[...].astype(BF16)


def _cast_leading_columns(w, ncols, *, tn=512):
    depth, rows, _ = w.shape
    spec = pl.BlockSpec((None, rows, tn), lambda l, j: (l, 0, j))
    return pl.pallas_call(
        _cast_kernel,
        grid=(depth, ncols // tn),
        in_specs=[spec],
        out_specs=spec,
        out_shape=jax.ShapeDtypeStruct((depth, rows, ncols), BF16),
        compiler_params=_params("parallel", "parallel"),
        name="cast_w_ab",
    )(w)


def _in_proj_kernel(x_ref, g_ref, wab_ref, wc_ref, ws_ref, oa_ref, ob_ref, oc_ref, os_ref, h_scr):
    j = pl.program_id(1)

    @pl.when(j == 0)
    def _():
        h = _rms(x_ref[...], g_ref[...]).astype(BF16)
        h_scr[...] = h
        small = lax.dot_general(ws_ref[...], h, NT_DIMS, preferred_element_type=F32)
        for c in range(small.shape[1] // DN_CHUNK):
            os_ref[c] = small[:, c * DN_CHUNK:(c + 1) * DN_CHUNK]

    def project(w_ref, o_ref, dims):
        res = lax.dot_general(h_scr[...], w_ref[...], dims, preferred_element_type=F32)
        for c in range(o_ref.shape[0]):
            o_ref[c] = res[:, c * LANES:(c + 1) * LANES].astype(BF16)

    pl.when(j == 0)(lambda: project(wab_ref, oa_ref, NN_DIMS))
    pl.when((j >= 1) & (j < IN_STEPS_AB))(lambda: project(wab_ref, ob_ref, NN_DIMS))
    pl.when(j >= IN_STEPS_AB)(lambda: project(wc_ref, oc_ref, NT_DIMS))


def _in_proj(x2, gain, w_ab, w_c, w_small, layer, *, tm=1024):
    t = x2.shape[0]
    tn_ab = SLABS_A * LANES
    tn_c = (SLABS_C // IN_STEPS_C) * LANES
    clamp = lambda v, hi: jnp.minimum(jnp.maximum(v, 0), hi)
    return pl.pallas_call(
        _in_proj_kernel,
        grid=(t // tm, IN_STEPS_AB + IN_STEPS_C),
        in_specs=[
            pl.BlockSpec((tm, D_MODEL), lambda i, j: (i, 0)),
            pl.BlockSpec((None, 1, D_MODEL), lambda i, j: (layer, 0, 0)),
            pl.BlockSpec((None, D_MODEL, tn_ab), lambda i, j: (layer, 0, clamp(j, IN_STEPS_AB - 1))),
            pl.BlockSpec((None, tn_c, D_MODEL), lambda i, j: (layer, clamp(j - IN_STEPS_AB, IN_STEPS_C - 1), 0)),
            pl.BlockSpec((None, SMALL_ROWS, D_MODEL), lambda i, j: (layer, 0, 0)),
        ],
        out_specs=[
            pl.BlockSpec((SLABS_A, tm, LANES), lambda i, j: (0, i, 0)),
            pl.BlockSpec((SLABS_A, tm, LANES), lambda i, j: (clamp(j - 1, IN_STEPS_AB - 2), i, 0)),
            pl.BlockSpec((SLABS_C // IN_STEPS_C, tm, LANES), lambda i, j: (clamp(j - IN_STEPS_AB, IN_STEPS_C - 1), i, 0)),
            pl.BlockSpec((tm // DN_CHUNK, SMALL_ROWS, DN_CHUNK), lambda i, j: (i, 0, 0)),
        ],
        out_shape=[
            jax.ShapeDtypeStruct((SLABS_A, t, LANES), BF16),
            jax.ShapeDtypeStruct((SLABS_B, t, LANES), BF16),
            jax.ShapeDtypeStruct((SLABS_C, t, LANES), BF16),
            jax.ShapeDtypeStruct((t // DN_CHUNK, SMALL_ROWS, DN_CHUNK), F32),
        ],
        scratch_shapes=[pltpu.VMEM((tm, D_MODEL), BF16)],
        compiler_params=_params("parallel", "arbitrary"),
        name="in_proj",
    )(x2, gain, w_ab, w_c, w_small)


def _mixer_a_kernel(p_ref, sg_ref, ws_ref, bs_ref, o_ref, *, chunks):
    ii = lax.broadcasted_iota(jnp.int32, (A_CHUNK, A_CHUNK), 0)
    jj = lax.broadcasted_iota(jnp.int32, (A_CHUNK, A_CHUNK), 1)
    for g in range(A_GROUPS):
        w = jnp.where(ii >= jj, ws_ref[g], 0.0).astype(BF16)
        gain = sg_ref[g:g + 1, :]
        bias = bs_ref[g]
        for c in range(chunks):
            rows = slice(c * A_CHUNK, (c + 1) * A_CHUNK)
            u = _gelu_tanh(p_ref[g, rows, :].astype(F32))
            v = _rms(_gelu_tanh(p_ref[A_GROUPS + g, rows, :].astype(F32)), gain)
            z = jnp.dot(w, v.astype(BF16), preferred_element_type=F32) + bias
            o_ref[g, rows, :] = (u * z).astype(BF16)


def _mixer_a(proj, sgu_g, w_s, b_s, *, tm=512):
    t = proj.shape[1]
    return pl.pallas_call(
        functools.partial(_mixer_a_kernel, chunks=tm // A_CHUNK),
        grid=(t // tm,),
        in_specs=[
            pl.BlockSpec((SLABS_A, tm, LANES), lambda i: (0, i, 0)),
            pl.BlockSpec((A_GROUPS, HEAD_DIM), lambda i: (0, 0)),
            pl.BlockSpec((A_GROUPS, A_CHUNK, A_CHUNK), lambda i: (0, 0, 0)),
            pl.BlockSpec((A_GROUPS, A_CHUNK, 1), lambda i: (0, 0, 0)),
        ],
        out_specs=pl.BlockSpec((A_GROUPS, tm, LANES), lambda i: (0, i, 0)),
        out_shape=jax.ShapeDtypeStruct((A_GROUPS, t, LANES), BF16),
        compiler_params=_params("parallel"),
        name="mixer_a",
    )(proj, sgu_g, w_s, b_s.reshape(A_GROUPS, A_CHUNK, 1))


def _unit_lower_inverse(a_list, ii, jj):
    def same_block(shift):
        return (ii >> shift) == (jj >> shift)

    a_pow = [jnp.where(same_block(4), a, 0.0) for a in a_list]
    eye = jnp.where(ii == jj, 1.0, 0.0)
    p = [eye - x for x in a_pow]
    for _ in range(3):
        a_pow = [_mm(x, x) for x in a_pow]
        p = [y + _mm(y, x) for y, x in zip(p, a_pow)]
    for shift in (4, 5, 6):
        off = same_block(shift + 1) & jnp.logical_not(same_block(shift))
        t = [_mm(jnp.where(off, a, 0.0), y) for a, y in zip(a_list, p)]
        p = [y - _mm(y, x) for y, x in zip(p, t)]
    return p


def _mixer_b_kernel(alog_ref, dtb_ref, q_ref, k_ref, v_ref, gate_ref, small_ref, cw_ref, og_ref, o_ref,
                    xpad, qn, kn, vn, gb_scr, el_scr, st_scr, u_scr, w_scr, at_scr, qg_scr, kdt_scr, *, rows_per_step):
    nchunks = rows_per_step // DN_CHUNK
    pad = 8
    sq = (DN_CHUNK, DN_CHUNK)

    @pl.when(pl.program_id(1) == 0)
    def _():
        xpad[:, 0:pad, :] = jnp.zeros((3 * B_HEADS, pad, HEAD_DIM), F32)
        st_scr[...] = jnp.zeros_like(st_scr)

    for kind, (src, dst) in enumerate(((q_ref, qn), (k_ref, kn), (v_ref, vn))):
        for h in range(B_HEADS):
            idx = kind * B_HEADS + h
            xpad[idx, pad:pad + rows_per_step, :] = src[h].astype(F32)
            cw = cw_ref[idx]
            for t in range(nchunks):
                base = pad + t * DN_CHUNK
                y = cw[B_CONV - 1:B_CONV, :] * xpad[idx, base:base + DN_CHUNK, :]
                for s in range(1, B_CONV):
                    y = y + cw[B_CONV - 1 - s:B_CONV - s, :] * xpad[idx, base - s:base - s + DN_CHUNK, :]
                y = _silu(y)
                if kind < 2:
                    y = y * lax.rsqrt(jnp.sum(y * y, axis=-1, keepdims=True) + EPS)
                if kind == 0:
                    y = y * (HEAD_DIM ** -0.5)
                dst[h, t * DN_CHUNK:(t + 1) * DN_CHUNK, :] = y
            xpad[idx, 0:pad, :] = xpad[idx, rows_per_step:rows_per_step + pad, :]

    small = small_ref[...].reshape(nchunks * SMALL_ROWS, DN_CHUNK)
    j = lax.broadcasted_iota(jnp.int32, small.shape, 0) % SMALL_ROWS
    lane = lax.broadcasted_iota(jnp.int32, small.shape, 1)
    dtb = jnp.zeros_like(small)
    alog = jnp.zeros_like(small)
    for h in range(B_HEADS):
        dtb = jnp.where(j == B_HEADS + h, dtb_ref[h], dtb)
        alog = jnp.where(j == B_HEADS + h, alog_ref[h], alog)
    xa = small + dtb
    g = -jnp.exp(alog) * (jnp.maximum(xa, 0.0) + jnp.log(1.0 + jnp.exp(-jnp.abs(xa))))
    shift = 1
    while shift < DN_CHUNK:
        g = g + jnp.where(lane >= shift, pltpu.roll(g, shift, 1), 0.0)
        shift *= 2
    gb_scr[...] = jnp.where(j < B_HEADS, _sigmoid(small), g)

    ii = lax.broadcasted_iota(jnp.int32, sq, 0)
    jj = lax.broadcasted_iota(jnp.int32, sq, 1)

    for c0 in range(0, nchunks, DN_GROUP):
        items = [(c, h) for c in range(c0, c0 + DN_GROUP) for h in range(B_HEADS)]
        rows = {c: slice(c * DN_CHUNK, (c + 1) * DN_CHUNK) for c, _ in items}
        g_j = [jnp.broadcast_to(gb_scr[c * SMALL_ROWS + B_HEADS + h:c * SMALL_ROWS + B_HEADS + h + 1, :], sq)
               for c, h in items]
        b_i = [jnp.broadcast_to(gb_scr[c * SMALL_ROWS + h:c * SMALL_ROWS + h + 1, :], sq).T
               for c, h in items]
        g_i = [x.T for x in g_j]
        decay = [jnp.exp(jnp.where(ii >= jj, x - y, -jnp.inf)) for x, y in zip(g_i, g_j)]
        kb = [kn[h, rows[c], :] * b for (c, h), b in zip(items, b_i)]
        a = [jnp.where(ii > jj, _mm_nt(x, kn[h, rows[c], :]) * d, 0.0) for x, (c, h), d in zip(kb, items, decay)]
        t_inv = _unit_lower_inverse(a, ii, jj)
        e_i = [jnp.exp(x) for x in g_i]
        sol = [_mm(t, jnp.concatenate([vn[h, rows[c], :] * b, x * e], axis=1))
               for t, (c, h), b, x, e in zip(t_inv, items, b_i, kb, e_i)]
        attn = [_mm_nt(qn[h, rows[c], :], kn[h, rows[c], :]) * d for (c, h), d in zip(items, decay)]
        for i, (c, h) in enumerate(items):
            r = rows[c]
            g_last = jnp.broadcast_to(g_j[i][:, DN_CHUNK - 1:DN_CHUNK], sq)
            u_scr[h, r, :] = sol[i][:, :HEAD_DIM]
            w_scr[h, r, :] = sol[i][:, HEAD_DIM:].astype(BF16)
            at_scr[h, r, :] = attn[i].astype(BF16)
            qg_scr[h, r, :] = (qn[h, r, :] * e_i[i]).astype(BF16)
            kdt_scr[h, r, :] = (kn[h, r, :] * jnp.exp(g_last - g_i[i])).T.astype(BF16)
            el_scr[h, c:c + 1, :] = jnp.exp(g_last[0:1, :])

    heads = range(B_HEADS)
    state = [st_scr[h] for h in heads]
    for c in range(nchunks):
        r = slice(c * DN_CHUNK, (c + 1) * DN_CHUNK)
        sb = [s.astype(BF16) for s in state]
        v_new = [u_scr[h, r, :] - jnp.dot(w_scr[h, r, :], sb[h], preferred_element_type=F32) for h in heads]
        vb = [x.astype(BF16) for x in v_new]
        o = [jnp.dot(qg_scr[h, r, :], sb[h], preferred_element_type=F32)
             + jnp.dot(at_scr[h, r, :], vb[h], preferred_element_type=F32) for h in heads]
        state = [state[h] * el_scr[h, c:c + 1, :] + jnp.dot(kdt_scr[h, r, :], vb[h], preferred_element_type=F32)
                 for h in heads]
        for h in heads:
            y = _rms(o[h], og_ref[...]) * _silu(gate_ref[h, r, :].astype(F32))
            o_ref[h, r, :] = y.astype(BF16)
    for h in heads:
        st_scr[h] = state[h]


def _mixer_b(proj, small3, conv_w3, a_log, dt_bias, o_norm_g, *, batch, seq):
    t = proj.shape[1]
    steps = seq // DN_STEP
    nch = DN_STEP // DN_CHUNK
    slab = lambda n: pl.BlockSpec((B_HEADS, DN_STEP, LANES), lambda b, s: (n, b * steps + s, 0))
    smem = pl.BlockSpec(memory_space=pltpu.SMEM)
    per_head = lambda rows, dtype: pltpu.VMEM((B_HEADS, rows, HEAD_DIM), dtype)
    return pl.pallas_call(
        functools.partial(_mixer_b_kernel, rows_per_step=DN_STEP),
        grid=(batch, steps),
        in_specs=[
            smem, smem,
            slab(0), slab(1), slab(2), slab(3),
            pl.BlockSpec((nch, SMALL_ROWS, DN_CHUNK), lambda b, s: (b * steps + s, 0, 0)),
            pl.BlockSpec((3 * B_HEADS, B_CONV, HEAD_DIM), lambda b, s: (0, 0, 0)),
            pl.BlockSpec((1, HEAD_DIM), lambda b, s: (0, 0)),
        ],
        out_specs=pl.BlockSpec((B_HEADS, DN_STEP, LANES), lambda b, s: (0, b * steps + s, 0)),
        out_shape=jax.ShapeDtypeStruct((B_HEADS, t, LANES), BF16),
        scratch_shapes=[
            pltpu.VMEM((3 * B_HEADS, DN_STEP + 8, HEAD_DIM), F32),
            per_head(DN_STEP, F32),
            per_head(DN_STEP, F32),
            per_head(DN_STEP, F32),
            pltpu.VMEM((nch * SMALL_ROWS, DN_CHUNK), F32),
            pltpu.VMEM((B_HEADS, nch, LANES), F32),
            per_head(HEAD_DIM, F32),
            per_head(DN_STEP, F32),
            per_head(DN_STEP, BF16),
            per_head(DN_STEP, BF16),
            per_head(DN_STEP, BF16),
            per_head(DN_STEP, BF16),
        ],
        compiler_params=_params("parallel", "arbitrary"),
        name="mixer_b",
    )(a_log, dt_bias, proj, proj, proj, proj, small3, conv_w3, o_norm_g)


def _mixer_c_kernel(slope_ref, q_ref, k_ref, v_ref, qg_ref, kg_ref, o_ref,
                    qs, ks, vs, ob, lb, *, seq):
    h = pl.program_id(1)
    slope = slope_ref[h]
    qs[...] = _rms(q_ref[...].astype(F32), qg_ref[...]) * (HEAD_DIM ** -0.5)
    ks[...] = _rms(k_ref[...].astype(F32), kg_ref[...])
    vs[...] = v_ref[...].astype(F32)

    qi = lax.broadcasted_iota(jnp.int32, (C_BLOCK, 2 * C_BLOCK), 0)
    kj = lax.broadcasted_iota(jnp.int32, (C_BLOCK, 2 * C_BLOCK), 1)
    delta = C_BLOCK + qi - kj

    for br, (window, dil) in enumerate(C_BRANCHES):
        span = window // dil
        seg = seq // dil
        in_band = (delta >= 0) & (delta <= span)
        bias = jnp.where(in_band, -slope * (delta * dil).astype(F32), -jnp.inf)
        bias_cur = bias[:, C_BLOCK:]

        def rows(start, size, dil=dil):
            return pl.ds(start, size, stride=dil) if dil > 1 else pl.ds(start, size)

        def attend(items, br=br):
            s = [_mm_nt(qs[q, :], ks[k, :]) + b for q, k, b in items]
            mx = [jnp.max(x, axis=-1, keepdims=True) for x in s]
            p = [jnp.exp(x - m).astype(BF16) for x, m in zip(s, mx)]
            acc = [jnp.dot(x, jnp.concatenate([vs[k, :].astype(BF16), jnp.ones((x.shape[1], LANES), BF16)], axis=1),
                           preferred_element_type=F32) for x, (_, k, _) in zip(p, items)]
            for (q, _, _), y, m in zip(items, acc, mx):
                den = y[:, HEAD_DIM:]
                ob[br, q, :] = y[:, :HEAD_DIM] / den
                lb[br, q, :] = m + jnp.log(den)

        nb = seg // C_BLOCK
        blocks = []
        for r in range(dil):
            blocks.append((rows(r, C_BLOCK), rows(r, C_BLOCK), bias_cur))
            for n in range(1, nb):
                blocks.append((rows(r + n * C_BLOCK * dil, C_BLOCK),
                               rows(r + (n - 1) * C_BLOCK * dil, 2 * C_BLOCK), bias))
        for g0 in range(0, len(blocks), C_GROUP):
            attend(blocks[g0:g0 + C_GROUP])

    def merge(n, carry):
        r = pl.ds(pl.multiple_of(n * C_BLOCK, C_BLOCK), C_BLOCK)
        l0, l1, l2 = lb[0, r, :], lb[1, r, :], lb[2, r, :]
        mx = jnp.maximum(jnp.maximum(l0, l1), l2)
        w0, w1, w2 = jnp.exp(l0 - mx), jnp.exp(l1 - mx), jnp.exp(l2 - mx)
        out = (w0 * ob[0, r, :] + w1 * ob[1, r, :] + w2 * ob[2, r, :]) / (w0 + w1 + w2)
        o_ref[r, :] = out.astype(BF16)
        return carry

    lax.fori_loop(0, seq // C_BLOCK, merge, 0)


def _mixer_c(proj, slopes, q_g, k_g, *, batch, seq):
    t = proj.shape[1]
    slab = lambda n: pl.BlockSpec((None, seq, LANES), lambda b, h: (n * C_HEADS + h, b, 0))
    vec = pl.BlockSpec((1, HEAD_DIM), lambda b, h: (0, 0))
    return pl.pallas_call(
        functools.partial(_mixer_c_kernel, seq=seq),
        grid=(batch, C_HEADS),
        in_specs=[pl.BlockSpec(memory_space=pltpu.SMEM),
                  slab(0), slab(1), slab(2), vec, vec],
        out_specs=pl.BlockSpec((None, seq, LANES), lambda b, h: (h, b, 0)),
        out_shape=jax.ShapeDtypeStruct((C_HEADS, t, LANES), BF16),
        scratch_shapes=[
            pltpu.VMEM((seq, HEAD_DIM), F32),
            pltpu.VMEM((seq, HEAD_DIM), F32),
            pltpu.VMEM((seq, HEAD_DIM), F32),
            pltpu.VMEM((len(C_BRANCHES), seq, HEAD_DIM), F32),
            pltpu.VMEM((len(C_BRANCHES), seq, LANES), F32),
        ],
        compiler_params=_params("parallel", "arbitrary"),
        name="mixer_c",
    )(slopes, proj, proj, proj, q_g, k_g)


def _out_proj_kernel(x_ref, ya_ref, yb_ref, yc_ref, w_ref, g_ref, xo_ref, ho_ref):
    mix = jnp.concatenate([ya_ref[i] for i in range(A_GROUPS)]
                          + [yb_ref[i] for i in range(B_HEADS)]
                          + [yc_ref[i] for i in range(C_HEADS)], axis=1)
    xn = x_ref[...] + jnp.dot(mix, w_ref[...], preferred_element_type=F32)
    xo_ref[...] = xn
    ho_ref[...] = _rms(xn, g_ref[...]).astype(BF16)


def _out_proj(x2, ya, yb, yc, w, gain, layer, *, tm=512):
    t = x2.shape[0]
    return pl.pallas_call(
        _out_proj_kernel,
        grid=(t // tm,),
        in_specs=[
            pl.BlockSpec((tm, D_MODEL), lambda i: (i, 0)),
            pl.BlockSpec((A_GROUPS, tm, LANES), lambda i: (0, i, 0)),
            pl.BlockSpec((B_HEADS, tm, LANES), lambda i: (0, i, 0)),
            pl.BlockSpec((C_HEADS, tm, LANES), lambda i: (0, i, 0)),
            pl.BlockSpec((None, D_MODEL, D_MODEL), lambda i: (layer, 0, 0), pipeline_mode=pl.Buffered(1)),
            pl.BlockSpec((None, 1, D_MODEL), lambda i: (layer, 0, 0)),
        ],
        out_specs=[
            pl.BlockSpec((tm, D_MODEL), lambda i: (i, 0)),
            pl.BlockSpec((tm, D_MODEL), lambda i: (i, 0)),
        ],
        out_shape=[
            jax.ShapeDtypeStruct((t, D_MODEL), F32),
            jax.ShapeDtypeStruct((t, D_MODEL), BF16),
        ],
        compiler_params=_params("parallel"),
        name="out_proj",
    )(x2, ya, yb, yc, w, gain)


def _ffn_kernel(h_ref, x_ref, wg_ref, wu_ref, wd_ref, o_ref):
    @pl.when(pl.program_id(1) == 0)
    def _():
        o_ref[...] = x_ref[...]

    h = h_ref[...]
    gt = jnp.dot(h, wg_ref[...], preferred_element_type=F32)
    up = jnp.dot(h, wu_ref[...], preferred_element_type=F32)
    act = (_silu(gt) * up).astype(BF16)
    o_ref[...] += jnp.dot(act, wd_ref[...], preferred_element_type=F32)


def _ffn(h2, x2, w_gate_up, w_down, layer, *, tm=512, tf=512):
    t = x2.shape[0]
    nf = FFN_HIDDEN // tf
    return pl.pallas_call(
        _ffn_kernel,
        grid=(t // tm, nf),
        in_specs=[
            pl.BlockSpec((tm, D_MODEL), lambda i, j: (i, 0)),
            pl.BlockSpec((tm, D_MODEL), lambda i, j: (i, 0)),
            pl.BlockSpec((None, D_MODEL, tf), lambda i, j: (layer, 0, j)),
            pl.BlockSpec((None, D_MODEL, tf), lambda i, j: (layer, 0, nf + j)),
            pl.BlockSpec((None, tf, D_MODEL), lambda i, j: (layer, j, 0)),
        ],
        out_specs=pl.BlockSpec((tm, D_MODEL), lambda i, j: (i, 0)),
        out_shape=jax.ShapeDtypeStruct((t, D_MODEL), F32),
        compiler_params=_params("parallel", "arbitrary"),
        name="ffn",
    )(h2, x2, w_gate_up, w_gate_up, w_down)


def kernel(x, norm1_g, w_in, sgu_norm_g, w_spatial, b_spatial, conv_w, a_log, dt_bias, o_norm_g,
           q_norm_g, k_norm_g, w_out, norm2_g, w_gate_up, w_down):
    batch, seq, _ = x.shape
    depth = w_in.shape[0]
    t = batch * seq
    slopes = jnp.exp2(-8.0 * (jnp.arange(C_HEADS, dtype=F32) + 1.0) / C_HEADS)
    w_ab = _cast_leading_columns(w_in, OFF_BETA)
    w_tail_t = jnp.swapaxes(w_in[:, :, OFF_BETA:], 1, 2).astype(BF16)
    w_c = w_tail_t[:, 2 * B_HEADS:, :]
    w_small = jnp.pad(w_tail_t[:, :2 * B_HEADS, :], ((0, 0), (0, SMALL_ROWS - 2 * B_HEADS), (0, 0)))
    w_out_b = w_out.astype(BF16)
    w_gate_up_b = w_gate_up.astype(BF16)
    w_down_b = w_down.astype(BF16)
    conv_w3 = conv_w.reshape(depth, B_CONV, 3 * B_HEADS, HEAD_DIM).transpose(0, 2, 1, 3)
    norm1 = norm1_g.reshape(depth, 1, D_MODEL)
    norm2 = norm2_g.reshape(depth, 1, D_MODEL)

    x2 = x.reshape(t, D_MODEL)
    for l in range(depth):
        pa, pb, pc, small3 = _in_proj(x2, norm1, w_ab, w_c, w_small, l)
        ya = _mixer_a(pa, sgu_norm_g[l], w_spatial[l], b_spatial[l])
        yb = _mixer_b(pb, small3, conv_w3[l], a_log[l], dt_bias[l], o_norm_g[l].reshape(1, HEAD_DIM),
                      batch=batch, seq=seq)
        yc = _mixer_c(pc, slopes, q_norm_g[l].reshape(1, HEAD_DIM), k_norm_g[l].reshape(1, HEAD_DIM),
                      batch=batch, seq=seq)
        x2, h2 = _out_proj(x2, ya, yb, yc, w_out_b, norm2, l)
        x2 = _ffn(h2, x2, w_gate_up_b, w_down_b, l)
    return x2.reshape(batch, seq, D_MODEL)
```

```python
import functools

import jax
import jax.numpy as jnp
from jax import lax
from jax.experimental import pallas as pl
from jax.experimental.pallas import tpu as pltpu

F32 = jnp.float32
BF16 = jnp.bfloat16

D_MODEL = 2048
HEAD_DIM = 128
A_GROUPS = 4
A_WIDTH = A_GROUPS * HEAD_DIM
A_CHUNK = 128
B_HEADS = 6
B_WIDTH = B_HEADS * HEAD_DIM
B_CONV = 4
C_HEADS = 6
C_WIDTH = C_HEADS * HEAD_DIM
C_BRANCHES = ((128, 1), (512, 4), (2048, 16))
C_BLOCK = 128
FFN_HIDDEN = 5632
EPS = 1e-6

OFF_B = 2 * A_WIDTH
OFF_BETA = OFF_B + 4 * B_WIDTH
OFF_C = OFF_BETA + 2 * B_HEADS
IN_TOTAL = OFF_C + 3 * C_WIDTH
SLABS_A, SLABS_B, SLABS_C = 2 * A_GROUPS, 4 * B_HEADS, 3 * C_HEADS
SMALL_ROWS = 16
IN_STEPS_AB = (SLABS_A + SLABS_B) // SLABS_A
IN_STEPS_C = 3
IN_NORM_PIECES = 4

LANES = 128
DN_CHUNK = 128
DN_STEP = 512
DN_GROUP = 2
C_GROUP = 8
VMEM_LIMIT = 56 * 1024 * 1024

NT_DIMS = (((1,), (1,)), ((), ()))
NN_DIMS = (((1,), (0,)), ((), ()))


def _mm(a, b):
    return jnp.dot(a.astype(BF16), b.astype(BF16), preferred_element_type=F32)


def _mm_nt(a, b):
    return lax.dot_general(a.astype(BF16), b.astype(BF16), NT_DIMS, preferred_element_type=F32)


def _sigmoid(x):
    return 0.5 * (1.0 + jnp.tanh(0.5 * x))


def _silu(x):
    return x * _sigmoid(x)


def _gelu_tanh(x):
    return 0.5 * x * (1.0 + jnp.tanh(0.7978845608028654 * (x + 0.044715 * (x * x * x))))


def _rms(x, gain):
    return x * lax.rsqrt(jnp.mean(x * x, axis=-1, keepdims=True) + EPS) * gain


def _params(*sem):
    return pltpu.CompilerParams(dimension_semantics=sem, vmem_limit_bytes=VMEM_LIMIT)


def _cast_kernel(w_ref, o_ref):
    o_ref[...] = w_ref[...].astype(BF16)


def _cast_w_in(w, *, tn=512):
    depth, rows, cols = w.shape
    spec = pl.BlockSpec((None, rows, tn), lambda l, j: (l, 0, j))
    return pl.pallas_call(
        _cast_kernel,
        grid=(depth, pl.cdiv(cols, tn)),
        in_specs=[spec],
        out_specs=spec,
        out_shape=jax.ShapeDtypeStruct((depth, rows, cols), BF16),
        compiler_params=_params("parallel", "parallel"),
        name="cast_w_in",
    )(w)


def _in_proj_kernel(x_ref, g_ref, wab_ref, wc_ref, ws_ref, oa_ref, ob_ref, oc_ref, os_ref, h_scr):
    j = pl.program_id(1)

    @pl.when(j == 0)
    def _():
        piece = x_ref.shape[0] // IN_NORM_PIECES
        for r in range(IN_NORM_PIECES):
            rows = slice(r * piece, (r + 1) * piece)
            h = _rms(x_ref[rows, :], g_ref[...]).astype(BF16)
            h_scr[rows, :] = h
            res = jnp.dot(h, wab_ref[...], preferred_element_type=F32)
            for c in range(oa_ref.shape[0]):
                oa_ref[c, rows, :] = res[:, c * LANES:(c + 1) * LANES].astype(BF16)
            small = lax.dot_general(ws_ref[...], h, NT_DIMS, preferred_element_type=F32)
            for c in range(piece // DN_CHUNK):
                os_ref[r * (piece // DN_CHUNK) + c] = small[:, c * DN_CHUNK:(c + 1) * DN_CHUNK]

    def project(w_ref, o_ref, dims):
        res = lax.dot_general(h_scr[...], w_ref[...], dims, preferred_element_type=F32)
        for c in range(o_ref.shape[0]):
            o_ref[c] = res[:, c * LANES:(c + 1) * LANES].astype(BF16)

    pl.when((j >= 1) & (j < IN_STEPS_AB))(lambda: project(wab_ref, ob_ref, NN_DIMS))
    pl.when(j >= IN_STEPS_AB)(lambda: project(wc_ref, oc_ref, NT_DIMS))


def _in_proj(x2, gain, w_ab, w_c, w_small, layer, *, tm=1024):
    t = x2.shape[0]
    tn_ab = SLABS_A * LANES
    tn_c = (SLABS_C // IN_STEPS_C) * LANES
    clamp = lambda v, hi: jnp.minimum(jnp.maximum(v, 0), hi)
    return pl.pallas_call(
        _in_proj_kernel,
        grid=(t // tm, IN_STEPS_AB + IN_STEPS_C),
        in_specs=[
            pl.BlockSpec((tm, D_MODEL), lambda i, j: (i, 0)),
            pl.BlockSpec((None, 1, D_MODEL), lambda i, j: (layer, 0, 0)),
            pl.BlockSpec((None, D_MODEL, tn_ab), lambda i, j: (layer, 0, clamp(j, IN_STEPS_AB - 1))),
            pl.BlockSpec((None, tn_c, D_MODEL), lambda i, j: (layer, clamp(j - IN_STEPS_AB, IN_STEPS_C - 1), 0)),
            pl.BlockSpec((None, SMALL_ROWS, D_MODEL), lambda i, j: (layer, 0, 0)),
        ],
        out_specs=[
            pl.BlockSpec((SLABS_A, tm, LANES), lambda i, j: (0, i, 0)),
            pl.BlockSpec((SLABS_A, tm, LANES), lambda i, j: (clamp(j - 1, IN_STEPS_AB - 2), i, 0)),
            pl.BlockSpec((SLABS_C // IN_STEPS_C, tm, LANES), lambda i, j: (clamp(j - IN_STEPS_AB, IN_STEPS_C - 1), i, 0)),
            pl.BlockSpec((tm // DN_CHUNK, SMALL_ROWS, DN_CHUNK), lambda i, j: (i, 0, 0)),
        ],
        out_shape=[
            jax.ShapeDtypeStruct((SLABS_A, t, LANES), BF16),
            jax.ShapeDtypeStruct((SLABS_B, t, LANES), BF16),
            jax.ShapeDtypeStruct((SLABS_C, t, LANES), BF16),
            jax.ShapeDtypeStruct((t // DN_CHUNK, SMALL_ROWS, DN_CHUNK), F32),
        ],
        scratch_shapes=[pltpu.VMEM((tm, D_MODEL), BF16)],
        compiler_params=_params("parallel", "arbitrary"),
        name="in_proj",
    )(x2, gain, w_ab, w_c, w_small)


def _mixer_a_kernel(p_ref, sg_ref, ws_ref, bs_ref, o_ref, *, chunks):
    ii = lax.broadcasted_iota(jnp.int32, (A_CHUNK, A_CHUNK), 0)
    jj = lax.broadcasted_iota(jnp.int32, (A_CHUNK, A_CHUNK), 1)
    for g in range(A_GROUPS):
        w = jnp.where(ii >= jj, ws_ref[g], 0.0).astype(BF16)
        gain = sg_ref[g:g + 1, :]
        bias = bs_ref[g]
        for c in range(chunks):
            rows = slice(c * A_CHUNK, (c + 1) * A_CHUNK)
            u = _gelu_tanh(p_ref[g, rows, :].astype(F32))
            v = _rms(_gelu_tanh(p_ref[A_GROUPS + g, rows, :].astype(F32)), gain)
            z = jnp.dot(w, v.astype(BF16), preferred_element_type=F32) + bias
            o_ref[g, rows, :] = (u * z).astype(BF16)


def _mixer_a(proj, sgu_g, w_s, b_s, *, tm=512):
    t = proj.shape[1]
    return pl.pallas_call(
        functools.partial(_mixer_a_kernel, chunks=tm // A_CHUNK),
        grid=(t // tm,),
        in_specs=[
            pl.BlockSpec((SLABS_A, tm, LANES), lambda i: (0, i, 0)),
            pl.BlockSpec((A_GROUPS, HEAD_DIM), lambda i: (0, 0)),
            pl.BlockSpec((A_GROUPS, A_CHUNK, A_CHUNK), lambda i: (0, 0, 0)),
            pl.BlockSpec((A_GROUPS, A_CHUNK, 1), lambda i: (0, 0, 0)),
        ],
        out_specs=pl.BlockSpec((A_GROUPS, tm, LANES), lambda i: (0, i, 0)),
        out_shape=jax.ShapeDtypeStruct((A_GROUPS, t, LANES), BF16),
        compiler_params=_params("parallel"),
        name="mixer_a",
    )(proj, sgu_g, w_s, b_s.reshape(A_GROUPS, A_CHUNK, 1))


def _unit_lower_inverse(a_list, ii, jj):
    def same_block(shift):
        return (ii >> shift) == (jj >> shift)

    a_pow = [jnp.where(same_block(4), a, 0.0) for a in a_list]
    eye = jnp.where(ii == jj, 1.0, 0.0)
    p = [eye - x for x in a_pow]
    for _ in range(3):
        a_pow = [_mm(x, x) for x in a_pow]
        p = [y + _mm(y, x) for y, x in zip(p, a_pow)]
    for shift in (4, 5, 6):
        off = same_block(shift + 1) & jnp.logical_not(same_block(shift))
        t = [_mm(jnp.where(off, a, 0.0), y) for a, y in zip(a_list, p)]
        p = [y - _mm(y, x) for y, x in zip(p, t)]
    return p


def _mixer_b_kernel(alog_ref, dtb_ref, q_ref, k_ref, v_ref, gate_ref, small_ref, cw_ref, og_ref, o_ref,
                    xpad, qn, kn, vn, gb_scr, el_scr, st_scr, u_scr, w_scr, at_scr, qg_scr, kdt_scr, *, rows_per_step):
    nchunks = rows_per_step // DN_CHUNK
    pad = 8
    sq = (DN_CHUNK, DN_CHUNK)

    @pl.when(pl.program_id(1) == 0)
    def _():
        xpad[:, 0:pad, :] = jnp.zeros((3 * B_HEADS, pad, HEAD_DIM), F32)
        st_scr[...] = jnp.zeros_like(st_scr)

    for kind, (src, dst) in enumerate(((q_ref, qn), (k_ref, kn), (v_ref, vn))):
        for h in range(B_HEADS):
            idx = kind * B_HEADS + h
            xpad[idx, pad:pad + rows_per_step, :] = src[h].astype(F32)
            cw = cw_ref[idx]
            for t in range(nchunks):
                base = pad + t * DN_CHUNK
                y = cw[B_CONV - 1:B_CONV, :] * xpad[idx, base:base + DN_CHUNK, :]
                for s in range(1, B_CONV):
                    y = y + cw[B_CONV - 1 - s:B_CONV - s, :] * xpad[idx, base - s:base - s + DN_CHUNK, :]
                y = _silu(y)
                if kind < 2:
                    y = y * lax.rsqrt(jnp.sum(y * y, axis=-1, keepdims=True) + EPS)
                if kind == 0:
                    y = y * (HEAD_DIM ** -0.5)
                dst[h, t * DN_CHUNK:(t + 1) * DN_CHUNK, :] = y
            xpad[idx, 0:pad, :] = xpad[idx, rows_per_step:rows_per_step + pad, :]

    small = small_ref[...].reshape(nchunks * SMALL_ROWS, DN_CHUNK)
    j = lax.broadcasted_iota(jnp.int32, small.shape, 0) % SMALL_ROWS
    lane = lax.broadcasted_iota(jnp.int32, small.shape, 1)
    dtb = jnp.zeros_like(small)
    alog = jnp.zeros_like(small)
    for h in range(B_HEADS):
        dtb = jnp.where(j == B_HEADS + h, dtb_ref[h], dtb)
        alog = jnp.where(j == B_HEADS + h, alog_ref[h], alog)
    xa = small + dtb
    g = -jnp.exp(alog) * (jnp.maximum(xa, 0.0) + jnp.log(1.0 + jnp.exp(-jnp.abs(xa))))
    shift = 1
    while shift < DN_CHUNK:
        g = g + jnp.where(lane >= shift, pltpu.roll(g, shift, 1), 0.0)
        shift *= 2
    gb_scr[...] = jnp.where(j < B_HEADS, _sigmoid(small), g)

    ii = lax.broadcasted_iota(jnp.int32, sq, 0)
    jj = lax.broadcasted_iota(jnp.int32, sq, 1)

    for c0 in range(0, nchunks, DN_GROUP):
        items = [(c, h) for c in range(c0, c0 + DN_GROUP) for h in range(B_HEADS)]
        rows = {c: slice(c * DN_CHUNK, (c + 1) * DN_CHUNK) for c, _ in items}
        g_j = [jnp.broadcast_to(gb_scr[c * SMALL_ROWS + B_HEADS + h:c * SMALL_ROWS + B_HEADS + h + 1, :], sq)
               for c, h in items]
        b_i = [jnp.broadcast_to(gb_scr[c * SMALL_ROWS + h:c * SMALL_ROWS + h + 1, :], sq).T
               for c, h in items]
        g_i = [x.T for x in g_j]
        decay = [jnp.exp(jnp.where(ii >= jj, x - y, -jnp.inf)) for x, y in zip(g_i, g_j)]
        kb = [kn[h, rows[c], :] * b for (c, h), b in zip(items, b_i)]
        a = [jnp.where(ii > jj, _mm_nt(x, kn[h, rows[c], :]) * d, 0.0) for x, (c, h), d in zip(kb, items, decay)]
        t_inv = _unit_lower_inverse(a, ii, jj)
        e_i = [jnp.exp(x) for x in g_i]
        sol = [_mm(t, jnp.concatenate([vn[h, rows[c], :] * b, x * e], axis=1))
               for t, (c, h), b, x, e in zip(t_inv, items, b_i, kb, e_i)]
        attn = [_mm_nt(qn[h, rows[c], :], kn[h, rows[c], :]) * d for (c, h), d in zip(items, decay)]
        for i, (c, h) in enumerate(items):
            r = rows[c]
            g_last = jnp.broadcast_to(g_j[i][:, DN_CHUNK - 1:DN_CHUNK], sq)
            u_scr[h, r, :] = sol[i][:, :HEAD_DIM]
            w_scr[h, r, :] = sol[i][:, HEAD_DIM:].astype(BF16)
            at_scr[h, r, :] = attn[i].astype(BF16)
            qg_scr[h, r, :] = (qn[h, r, :] * e_i[i]).astype(BF16)
            kdt_scr[h, r, :] = (kn[h, r, :] * jnp.exp(g_last - g_i[i])).T.astype(BF16)
            el_scr[h, c:c + 1, :] = jnp.exp(g_last[0:1, :])

    heads = range(B_HEADS)
    state = [st_scr[h] for h in heads]
    for c in range(nchunks):
        r = slice(c * DN_CHUNK, (c + 1) * DN_CHUNK)
        sb = [s.astype(BF16) for s in state]
        v_new = [u_scr[h, r, :] - jnp.dot(w_scr[h, r, :], sb[h], preferred_element_type=F32) for h in heads]
        vb = [x.astype(BF16) for x in v_new]
        o = [jnp.dot(qg_scr[h, r, :], sb[h], preferred_element_type=F32)
             + jnp.dot(at_scr[h, r, :], vb[h], preferred_element_type=F32) for h in heads]
        state = [state[h] * el_scr[h, c:c + 1, :] + jnp.dot(kdt_scr[h, r, :], vb[h], preferred_element_type=F32)
                 for h in heads]
        for h in heads:
            y = _rms(o[h], og_ref[...]) * _silu(gate_ref[h, r, :].astype(F32))
            o_ref[h, r, :] = y.astype(BF16)
    for h in heads:
        st_scr[h] = state[h]


def _mixer_b(proj, small3, conv_w3, a_log, dt_bias, o_norm_g, *, batch, seq):
    t = proj.shape[1]
    steps = seq // DN_STEP
    nch = DN_STEP // DN_CHUNK
    slab = lambda n: pl.BlockSpec((B_HEADS, DN_STEP, LANES), lambda b, s: (n, b * steps + s, 0))
    smem = pl.BlockSpec(memory_space=pltpu.SMEM)
    per_head = lambda rows, dtype: pltpu.VMEM((B_HEADS, rows, HEAD_DIM), dtype)
    return pl.pallas_call(
        functools.partial(_mixer_b_kernel, rows_per_step=DN_STEP),
        grid=(batch, steps),
        in_specs=[
            smem, smem,
            slab(0), slab(1), slab(2), slab(3),
            pl.BlockSpec((nch, SMALL_ROWS, DN_CHUNK), lambda b, s: (b * steps + s, 0, 0)),
            pl.BlockSpec((3 * B_HEADS, B_CONV, HEAD_DIM), lambda b, s: (0, 0, 0)),
            pl.BlockSpec((1, HEAD_DIM), lambda b, s: (0, 0)),
        ],
        out_specs=pl.BlockSpec((B_HEADS, DN_STEP, LANES), lambda b, s: (0, b * steps + s, 0)),
        out_shape=jax.ShapeDtypeStruct((B_HEADS, t, LANES), BF16),
        scratch_shapes=[
            pltpu.VMEM((3 * B_HEADS, DN_STEP + 8, HEAD_DIM), F32),
            per_head(DN_STEP, F32),
            per_head(DN_STEP, F32),
            per_head(DN_STEP, F32),
            pltpu.VMEM((nch * SMALL_ROWS, DN_CHUNK), F32),
            pltpu.VMEM((B_HEADS, nch, LANES), F32),
            per_head(HEAD_DIM, F32),
            per_head(DN_STEP, F32),
            per_head(DN_STEP, BF16),
            per_head(DN_STEP, BF16),
            per_head(DN_STEP, BF16),
            per_head(DN_STEP, BF16),
        ],
        compiler_params=_params("parallel", "arbitrary"),
        name="mixer_b",
    )(a_log, dt_bias, proj, proj, proj, proj, small3, conv_w3, o_norm_g)


def _mixer_c_kernel(slope_ref, q_ref, k_ref, v_ref, qg_ref, kg_ref, o_ref,
                    qs, ks, vs, ob, lb, *, seq):
    h = pl.program_id(1)
    slope = slope_ref[h]
    qs[...] = _rms(q_ref[...].astype(F32), qg_ref[...]) * (HEAD_DIM ** -0.5)
    ks[...] = _rms(k_ref[...].astype(F32), kg_ref[...])
    vs[...] = v_ref[...].astype(F32)

    qi = lax.broadcasted_iota(jnp.int32, (C_BLOCK, 2 * C_BLOCK), 0)
    kj = lax.broadcasted_iota(jnp.int32, (C_BLOCK, 2 * C_BLOCK), 1)
    delta = C_BLOCK + qi - kj

    for br, (window, dil) in enumerate(C_BRANCHES):
        span = window // dil
        seg = seq // dil
        in_band = (delta >= 0) & (delta <= span)
        bias = jnp.where(in_band, -slope * (delta * dil).astype(F32), -jnp.inf)
        bias_cur = bias[:, C_BLOCK:]

        def rows(start, size, dil=dil):
            return pl.ds(start, size, stride=dil) if dil > 1 else pl.ds(start, size)

        def attend(items, br=br):
            s = [_mm_nt(qs[q, :], ks[k, :]) + b for q, k, b in items]
            mx = [jnp.max(x, axis=-1, keepdims=True) for x in s]
            p = [jnp.exp(x - m).astype(BF16) for x, m in zip(s, mx)]
            acc = [jnp.dot(x, jnp.concatenate([vs[k, :].astype(BF16), jnp.ones((x.shape[1], LANES), BF16)], axis=1),
                           preferred_element_type=F32) for x, (_, k, _) in zip(p, items)]
            for (q, _, _), y, m in zip(items, acc, mx):
                den = y[:, HEAD_DIM:]
                ob[br, q, :] = y[:, :HEAD_DIM] / den
                lb[br, q, :] = m + jnp.log(den)

        nb = seg // C_BLOCK
        blocks = []
        for r in range(dil):
            blocks.append((rows(r, C_BLOCK), rows(r, C_BLOCK), bias_cur))
            for n in range(1, nb):
                blocks.append((rows(r + n * C_BLOCK * dil, C_BLOCK),
                               rows(r + (n - 1) * C_BLOCK * dil, 2 * C_BLOCK), bias))
        for g0 in range(0, len(blocks), C_GROUP):
            attend(blocks[g0:g0 + C_GROUP])

    def merge(n, carry):
        r = pl.ds(pl.multiple_of(n * C_BLOCK, C_BLOCK), C_BLOCK)
        l0, l1, l2 = lb[0, r, :], lb[1, r, :], lb[2, r, :]
        mx = jnp.maximum(jnp.maximum(l0, l1), l2)
        w0, w1, w2 = jnp.exp(l0 - mx), jnp.exp(l1 - mx), jnp.exp(l2 - mx)
        out = (w0 * ob[0, r, :] + w1 * ob[1, r, :] + w2 * ob[2, r, :]) / (w0 + w1 + w2)
        o_ref[r, :] = out.astype(BF16)
        return carry

    lax.fori_loop(0, seq // C_BLOCK, merge, 0)


def _mixer_c(proj, slopes, q_g, k_g, *, batch, seq):
    t = proj.shape[1]
    slab = lambda n: pl.BlockSpec((None, seq, LANES), lambda b, h: (n * C_HEADS + h, b, 0))
    vec = pl.BlockSpec((1, HEAD_DIM), lambda b, h: (0, 0))
    return pl.pallas_call(
        functools.partial(_mixer_c_kernel, seq=seq),
        grid=(batch, C_HEADS),
        in_specs=[pl.BlockSpec(memory_space=pltpu.SMEM),
                  slab(0), slab(1), slab(2), vec, vec],
        out_specs=pl.BlockSpec((None, seq, LANES), lambda b, h: (h, b, 0)),
        out_shape=jax.ShapeDtypeStruct((C_HEADS, t, LANES), BF16),
        scratch_shapes=[
            pltpu.VMEM((seq, HEAD_DIM), F32),
            pltpu.VMEM((seq, HEAD_DIM), F32),
            pltpu.VMEM((seq, HEAD_DIM), F32),
            pltpu.VMEM((len(C_BRANCHES), seq, HEAD_DIM), F32),
            pltpu.VMEM((len(C_BRANCHES), seq, LANES), F32),
        ],
        compiler_params=_params("parallel", "arbitrary"),
        name="mixer_c",
    )(slopes, proj, proj, proj, q_g, k_g)


def _out_proj_kernel(x_ref, ya_ref, yb_ref, yc_ref, w_ref, g_ref, xo_ref, ho_ref):
    mix = jnp.concatenate([ya_ref[i] for i in range(A_GROUPS)]
                          + [yb_ref[i] for i in range(B_HEADS)]
                          + [yc_ref[i] for i in range(C_HEADS)], axis=1)
    xn = x_ref[...] + jnp.dot(mix, w_ref[...], preferred_element_type=F32)
    xo_ref[...] = xn
    ho_ref[...] = _rms(xn, g_ref[...]).astype(BF16)


def _out_proj(x2, ya, yb, yc, w, gain, layer, *, tm=512):
    t = x2.shape[0]
    return pl.pallas_call(
        _out_proj_kernel,
        grid=(t // tm,),
        in_specs=[
            pl.BlockSpec((tm, D_MODEL), lambda i: (i, 0)),
            pl.BlockSpec((A_GROUPS, tm, LANES), lambda i: (0, i, 0)),
            pl.BlockSpec((B_HEADS, tm, LANES), lambda i: (0, i, 0)),
            pl.BlockSpec((C_HEADS, tm, LANES), lambda i: (0, i, 0)),
            pl.BlockSpec((None, D_MODEL, D_MODEL), lambda i: (layer, 0, 0), pipeline_mode=pl.Buffered(1)),
            pl.BlockSpec((None, 1, D_MODEL), lambda i: (layer, 0, 0)),
        ],
        out_specs=[
            pl.BlockSpec((tm, D_MODEL), lambda i: (i, 0)),
            pl.BlockSpec((tm, D_MODEL), lambda i: (i, 0)),
        ],
        out_shape=[
            jax.ShapeDtypeStruct((t, D_MODEL), F32),
            jax.ShapeDtypeStruct((t, D_MODEL), BF16),
        ],
        compiler_params=_params("parallel"),
        name="out_proj",
    )(x2, ya, yb, yc, w, gain)


def _ffn_kernel(h_ref, x_ref, wg_ref, wu_ref, wd_ref, o_ref):
    @pl.when(pl.program_id(1) == 0)
    def _():
        o_ref[...] = x_ref[...]

    h = h_ref[...]
    gt = jnp.dot(h, wg_ref[...], preferred_element_type=F32)
    up = jnp.dot(h, wu_ref[...], preferred_element_type=F32)
    act = (_silu(gt) * up).astype(BF16)
    o_ref[...] += jnp.dot(act, wd_ref[...], preferred_element_type=F32)


def _ffn(h2, x2, w_gate_up, w_down, layer, *, tm=512, tf=512):
    t = x2.shape[0]
    nf = FFN_HIDDEN // tf
    return pl.pallas_call(
        _ffn_kernel,
        grid=(t // tm, nf),
        in_specs=[
            pl.BlockSpec((tm, D_MODEL), lambda i, j: (i, 0)),
            pl.BlockSpec((tm, D_MODEL), lambda i, j: (i, 0)),
            pl.BlockSpec((None, D_MODEL, tf), lambda i, j: (layer, 0, j)),
            pl.BlockSpec((None, D_MODEL, tf), lambda i, j: (layer, 0, nf + j)),
            pl.BlockSpec((None, tf, D_MODEL), lambda i, j: (layer, j, 0)),
        ],
        out_specs=pl.BlockSpec((tm, D_MODEL), lambda i, j: (i, 0)),
        out_shape=jax.ShapeDtypeStruct((t, D_MODEL), F32),
        compiler_params=_params("parallel", "arbitrary"),
        name="ffn",
    )(h2, x2, w_gate_up, w_gate_up, w_down)


def kernel(x, norm1_g, w_in, sgu_norm_g, w_spatial, b_spatial, conv_w, a_log, dt_bias, o_norm_g,
           q_norm_g, k_norm_g, w_out, norm2_g, w_gate_up, w_down):
    batch, seq, _ = x.shape
    depth = w_in.shape[0]
    t = batch * seq
    slopes = jnp.exp2(-8.0 * (jnp.arange(C_HEADS, dtype=F32) + 1.0) / C_HEADS)
    w_in_b = _cast_w_in(w_in)
    w_tail_t = jnp.swapaxes(w_in_b[:, :, OFF_BETA:], 1, 2)
    w_c = w_tail_t[:, 2 * B_HEADS:, :]
    w_small = jnp.pad(w_tail_t[:, :2 * B_HEADS, :], ((0, 0), (0, SMALL_ROWS - 2 * B_HEADS), (0, 0)))
    w_out_b = w_out.astype(BF16)
    w_gate_up_b = w_gate_up.astype(BF16)
    w_down_b = w_down.astype(BF16)
    conv_w3 = conv_w.reshape(depth, B_CONV, 3 * B_HEADS, HEAD_DIM).transpose(0, 2, 1, 3)
    norm1 = norm1_g.reshape(depth, 1, D_MODEL)
    norm2 = norm2_g.reshape(depth, 1, D_MODEL)

    x2 = x.reshape(t, D_MODEL)
    for l in range(depth):
        pa, pb, pc, small3 = _in_proj(x2, norm1, w_in_b, w_c, w_small, l)
        ya = _mixer_a(pa, sgu_norm_g[l], w_spatial[l], b_spatial[l])
        yb = _mixer_b(pb, small3, conv_w3[l], a_log[l], dt_bias[l], o_norm_g[l].reshape(1, HEAD_DIM),
                      batch=batch, seq=seq)
        yc = _mixer_c(pc, slopes, q_norm_g[l].reshape(1, HEAD_DIM), k_norm_g[l].reshape(1, HEAD_DIM),
                      batch=batch, seq=seq)
        x2, h2 = _out_proj(x2, ya, yb, yc, w_out_b, norm2, l)
        x2 = _ffn(h2, x2, w_gate_up_b, w_down_b, l)
    return x2.reshape(batch, seq, D_MODEL)
```

```python
import functools

import jax
import jax.numpy as jnp
from jax import lax
from jax.experimental import pallas as pl
from jax.experimental.pallas import tpu as pltpu

F32 = jnp.float32
BF16 = jnp.bfloat16

D_MODEL = 2048
HEAD_DIM = 128
A_GROUPS = 4
A_WIDTH = A_GROUPS * HEAD_DIM
A_CHUNK = 128
B_HEADS = 6
B_WIDTH = B_HEADS * HEAD_DIM
B_CONV = 4
C_HEADS = 6
C_WIDTH = C_HEADS * HEAD_DIM
C_BRANCHES = ((128, 1), (512, 4), (2048, 16))
C_BLOCK = 128
FFN_HIDDEN = 5632
FFN_TILE = 512
EPS = 1e-6

OFF_B = 2 * A_WIDTH
OFF_BETA = OFF_B + 4 * B_WIDTH
OFF_C = OFF_BETA + 2 * B_HEADS
IN_TOTAL = OFF_C + 3 * C_WIDTH
SLABS_A, SLABS_B, SLABS_C = 2 * A_GROUPS, 4 * B_HEADS, 3 * C_HEADS
SMALL_ROWS = 16
IN_STEPS_AB = (SLABS_A + SLABS_B) // SLABS_A
IN_STEPS_C = 3
IN_NORM_PIECES = 4

LANES = 128
DN_CHUNK = 128
DN_STEP = 512
DN_GROUP = 2
C_GROUP = 8
VMEM_LIMIT = 56 * 1024 * 1024

NT_DIMS = (((1,), (1,)), ((), ()))


def _mm(a, b):
    return jnp.dot(a.astype(BF16), b.astype(BF16), preferred_element_type=F32)


def _mm_nt(a, b):
    return lax.dot_general(a.astype(BF16), b.astype(BF16), NT_DIMS, preferred_element_type=F32)


def _sigmoid(x):
    return 0.5 * (1.0 + jnp.tanh(0.5 * x))


def _silu(x):
    return x * _sigmoid(x)


def _gelu_tanh(x):
    return 0.5 * x * (1.0 + jnp.tanh(0.7978845608028654 * (x + 0.044715 * (x * x * x))))


def _rms(x, gain):
    return x * lax.rsqrt(jnp.mean(x * x, axis=-1, keepdims=True) + EPS) * gain


def _params(*sem):
    return pltpu.CompilerParams(dimension_semantics=sem, vmem_limit_bytes=VMEM_LIMIT)


def _in_proj_kernel(x_ref, g_ref, wab_ref, wc_ref, ws_ref, oa_ref, ob_ref, oc_ref, os_ref, h_scr):
    j = pl.program_id(1)

    @pl.when(j == 0)
    def _():
        piece = x_ref.shape[0] // IN_NORM_PIECES
        for r in range(IN_NORM_PIECES):
            rows = slice(r * piece, (r + 1) * piece)
            h = _rms(x_ref[rows, :], g_ref[...]).astype(BF16)
            h_scr[rows, :] = h
            res = lax.dot_general(h, wab_ref[...], NT_DIMS, preferred_element_type=F32)
            for c in range(oa_ref.shape[0]):
                oa_ref[c, rows, :] = res[:, c * LANES:(c + 1) * LANES].astype(BF16)
            small = lax.dot_general(ws_ref[...], h, NT_DIMS, preferred_element_type=F32)
            for c in range(piece // DN_CHUNK):
                os_ref[r * (piece // DN_CHUNK) + c] = small[:, c * DN_CHUNK:(c + 1) * DN_CHUNK]

    def project(w_ref, o_ref):
        res = lax.dot_general(h_scr[...], w_ref[...], NT_DIMS, preferred_element_type=F32)
        for c in range(o_ref.shape[0]):
            o_ref[c] = res[:, c * LANES:(c + 1) * LANES].astype(BF16)

    pl.when((j >= 1) & (j < IN_STEPS_AB))(lambda: project(wab_ref, ob_ref))
    pl.when(j >= IN_STEPS_AB)(lambda: project(wc_ref, oc_ref))


def _in_proj(x2, gain, w_ab, w_c, w_small, layer, *, tm=1024):
    t = x2.shape[0]
    tn_ab = SLABS_A * LANES
    tn_c = (SLABS_C // IN_STEPS_C) * LANES
    clamp = lambda v, hi: jnp.minimum(jnp.maximum(v, 0), hi)
    return pl.pallas_call(
        _in_proj_kernel,
        grid=(t // tm, IN_STEPS_AB + IN_STEPS_C),
        in_specs=[
            pl.BlockSpec((tm, D_MODEL), lambda i, j: (i, 0)),
            pl.BlockSpec((None, 1, D_MODEL), lambda i, j: (layer, 0, 0)),
            pl.BlockSpec((None, tn_ab, D_MODEL), lambda i, j: (layer, clamp(j, IN_STEPS_AB - 1), 0)),
            pl.BlockSpec((None, tn_c, D_MODEL), lambda i, j: (layer, clamp(j - IN_STEPS_AB, IN_STEPS_C - 1), 0)),
            pl.BlockSpec((None, SMALL_ROWS, D_MODEL), lambda i, j: (layer, 0, 0)),
        ],
        out_specs=[
            pl.BlockSpec((SLABS_A, tm, LANES), lambda i, j: (0, i, 0)),
            pl.BlockSpec((SLABS_A, tm, LANES), lambda i, j: (clamp(j - 1, IN_STEPS_AB - 2), i, 0)),
            pl.BlockSpec((SLABS_C // IN_STEPS_C, tm, LANES), lambda i, j: (clamp(j - IN_STEPS_AB, IN_STEPS_C - 1), i, 0)),
            pl.BlockSpec((tm // DN_CHUNK, SMALL_ROWS, DN_CHUNK), lambda i, j: (i, 0, 0)),
        ],
        out_shape=[
            jax.ShapeDtypeStruct((SLABS_A, t, LANES), BF16),
            jax.ShapeDtypeStruct((SLABS_B, t, LANES), BF16),
            jax.ShapeDtypeStruct((SLABS_C, t, LANES), BF16),
            jax.ShapeDtypeStruct((t // DN_CHUNK, SMALL_ROWS, DN_CHUNK), F32),
        ],
        scratch_shapes=[pltpu.VMEM((tm, D_MODEL), BF16)],
        compiler_params=_params("parallel", "arbitrary"),
        name="in_proj",
    )(x2, gain, w_ab, w_c, w_small)


def _mixer_a_kernel(p_ref, sg_ref, ws_ref, bs_ref, o_ref, *, chunks):
    ii = lax.broadcasted_iota(jnp.int32, (A_CHUNK, A_CHUNK), 0)
    jj = lax.broadcasted_iota(jnp.int32, (A_CHUNK, A_CHUNK), 1)
    for g in range(A_GROUPS):
        w = jnp.where(ii >= jj, ws_ref[g], 0.0).astype(BF16)
        gain = sg_ref[g:g + 1, :]
        bias = bs_ref[g]
        for c in range(chunks):
            rows = slice(c * A_CHUNK, (c + 1) * A_CHUNK)
            u = _gelu_tanh(p_ref[g, rows, :].astype(F32))
            v = _rms(_gelu_tanh(p_ref[A_GROUPS + g, rows, :].astype(F32)), gain)
            z = jnp.dot(w, v.astype(BF16), preferred_element_type=F32) + bias
            o_ref[g, rows, :] = (u * z).astype(BF16)


def _mixer_a(proj, sgu_g, w_s, b_s, *, tm=512):
    t = proj.shape[1]
    return pl.pallas_call(
        functools.partial(_mixer_a_kernel, chunks=tm // A_CHUNK),
        grid=(t // tm,),
        in_specs=[
            pl.BlockSpec((SLABS_A, tm, LANES), lambda i: (0, i, 0)),
            pl.BlockSpec((A_GROUPS, HEAD_DIM), lambda i: (0, 0)),
            pl.BlockSpec((A_GROUPS, A_CHUNK, A_CHUNK), lambda i: (0, 0, 0)),
            pl.BlockSpec((A_GROUPS, A_CHUNK, 1), lambda i: (0, 0, 0)),
        ],
        out_specs=pl.BlockSpec((A_GROUPS, tm, LANES), lambda i: (0, i, 0)),
        out_shape=jax.ShapeDtypeStruct((A_GROUPS, t, LANES), BF16),
        compiler_params=_params("parallel"),
        name="mixer_a",
    )(proj, sgu_g, w_s, b_s.reshape(A_GROUPS, A_CHUNK, 1))


def _unit_lower_inverse(a_list, ii, jj):
    def same_block(shift):
        return (ii >> shift) == (jj >> shift)

    a_pow = [jnp.where(same_block(4), a, 0.0) for a in a_list]
    eye = jnp.where(ii == jj, 1.0, 0.0)
    p = [eye - x for x in a_pow]
    for _ in range(3):
        a_pow = [_mm(x, x) for x in a_pow]
        p = [y + _mm(y, x) for y, x in zip(p, a_pow)]
    for shift in (4, 5, 6):
        off = same_block(shift + 1) & jnp.logical_not(same_block(shift))
        t = [_mm(jnp.where(off, a, 0.0), y) for a, y in zip(a_list, p)]
        p = [y - _mm(y, x) for y, x in zip(p, t)]
    return p


def _mixer_b_kernel(alog_ref, dtb_ref, q_ref, k_ref, v_ref, gate_ref, small_ref, cw_ref, og_ref, o_ref,
                    xpad, qn, kn, vn, gb_scr, el_scr, st_scr, u_scr, w_scr, at_scr, qg_scr, kdt_scr, *, rows_per_step):
    nchunks = rows_per_step // DN_CHUNK
    pad = 8
    sq = (DN_CHUNK, DN_CHUNK)

    @pl.when(pl.program_id(1) == 0)
    def _():
        xpad[:, 0:pad, :] = jnp.zeros((3 * B_HEADS, pad, HEAD_DIM), F32)
        st_scr[...] = jnp.zeros_like(st_scr)

    tensors = ((q_ref, qn), (k_ref, kn), (v_ref, vn))
    for kind, (src, _) in enumerate(tensors):
        for h in range(B_HEADS):
            xpad[kind * B_HEADS + h, pad:pad + rows_per_step, :] = src[h].astype(F32)

    def conv_chunks(chunks):
        for kind, (_, dst) in enumerate(tensors):
            for h in range(B_HEADS):
                idx = kind * B_HEADS + h
                cw = cw_ref[idx]
                for t in chunks:
                    base = pad + t * DN_CHUNK
                    y = cw[B_CONV - 1:B_CONV, :] * xpad[idx, base:base + DN_CHUNK, :]
                    for s in range(1, B_CONV):
                        y = y + cw[B_CONV - 1 - s:B_CONV - s, :] * xpad[idx, base - s:base - s + DN_CHUNK, :]
                    y = _silu(y)
                    if kind < 2:
                        y = y * lax.rsqrt(jnp.sum(y * y, axis=-1, keepdims=True) + EPS)
                    if kind == 0:
                        y = y * (HEAD_DIM ** -0.5)
                    dst[h, t * DN_CHUNK:(t + 1) * DN_CHUNK, :] = y

    small = small_ref[...].reshape(nchunks * SMALL_ROWS, DN_CHUNK)
    j = lax.broadcasted_iota(jnp.int32, small.shape, 0) % SMALL_ROWS
    lane = lax.broadcasted_iota(jnp.int32, small.shape, 1)
    dtb = jnp.zeros_like(small)
    alog = jnp.zeros_like(small)
    for h in range(B_HEADS):
        dtb = jnp.where(j == B_HEADS + h, dtb_ref[h], dtb)
        alog = jnp.where(j == B_HEADS + h, alog_ref[h], alog)
    xa = small + dtb
    g = -jnp.exp(alog) * (jnp.maximum(xa, 0.0) + jnp.log(1.0 + jnp.exp(-jnp.abs(xa))))
    shift = 1
    while shift < DN_CHUNK:
        g = g + jnp.where(lane >= shift, pltpu.roll(g, shift, 1), 0.0)
        shift *= 2
    gb_scr[...] = jnp.where(j < B_HEADS, _sigmoid(small), g)

    ii = lax.broadcasted_iota(jnp.int32, sq, 0)
    jj = lax.broadcasted_iota(jnp.int32, sq, 1)

    conv_chunks(range(DN_GROUP))
    for c0 in range(0, nchunks, DN_GROUP):
        items = [(c, h) for c in range(c0, c0 + DN_GROUP) for h in range(B_HEADS)]
        rows = {c: slice(c * DN_CHUNK, (c + 1) * DN_CHUNK) for c, _ in items}
        g_j = [jnp.broadcast_to(gb_scr[c * SMALL_ROWS + B_HEADS + h:c * SMALL_ROWS + B_HEADS + h + 1, :], sq)
               for c, h in items]
        b_i = [jnp.broadcast_to(gb_scr[c * SMALL_ROWS + h:c * SMALL_ROWS + h + 1, :], sq).T
               for c, h in items]
        g_i = [x.T for x in g_j]
        decay = [jnp.exp(jnp.where(ii >= jj, x - y, -jnp.inf)) for x, y in zip(g_i, g_j)]
        kb = [kn[h, rows[c], :] * b for (c, h), b in zip(items, b_i)]
        a = [jnp.where(ii > jj, _mm_nt(x, kn[h, rows[c], :]) * d, 0.0) for x, (c, h), d in zip(kb, items, decay)]
        t_inv = _unit_lower_inverse(a, ii, jj)
        e_i = [jnp.exp(x) for x in g_i]
        sol = [_mm(t, jnp.concatenate([vn[h, rows[c], :] * b, x * e], axis=1))
               for t, (c, h), b, x, e in zip(t_inv, items, b_i, kb, e_i)]
        attn = [_mm_nt(qn[h, rows[c], :], kn[h, rows[c], :]) * d for (c, h), d in zip(items, decay)]
        for i, (c, h) in enumerate(items):
            r = rows[c]
            g_last = jnp.broadcast_to(g_j[i][:, DN_CHUNK - 1:DN_CHUNK], sq)
            u_scr[h, r, :] = sol[i][:, :HEAD_DIM]
            w_scr[h, r, :] = sol[i][:, HEAD_DIM:].astype(BF16)
            at_scr[h, r, :] = attn[i].astype(BF16)
            qg_scr[h, r, :] = (qn[h, r, :] * e_i[i]).astype(BF16)
            kdt_scr[h, r, :] = (kn[h, r, :] * jnp.exp(g_last - g_i[i])).T.astype(BF16)
            el_scr[h, c:c + 1, :] = jnp.exp(g_last[0:1, :])
        if c0 + DN_GROUP < nchunks:
            conv_chunks(range(c0 + DN_GROUP, c0 + 2 * DN_GROUP))
    for idx in range(3 * B_HEADS):
        xpad[idx, 0:pad, :] = xpad[idx, rows_per_step:rows_per_step + pad, :]

    heads = range(B_HEADS)
    state = [st_scr[h] for h in heads]
    for c in range(nchunks):
        r = slice(c * DN_CHUNK, (c + 1) * DN_CHUNK)
        sb = [s.astype(BF16) for s in state]
        v_new = [u_scr[h, r, :] - jnp.dot(w_scr[h, r, :], sb[h], preferred_element_type=F32) for h in heads]
        vb = [x.astype(BF16) for x in v_new]
        o = [jnp.dot(qg_scr[h, r, :], sb[h], preferred_element_type=F32)
             + jnp.dot(at_scr[h, r, :], vb[h], preferred_element_type=F32) for h in heads]
        state = [state[h] * el_scr[h, c:c + 1, :] + jnp.dot(kdt_scr[h, r, :], vb[h], preferred_element_type=F32)
                 for h in heads]
        for h in heads:
            y = _rms(o[h], og_ref[...]) * _silu(gate_ref[h, r, :].astype(F32))
            o_ref[h, r, :] = y.astype(BF16)
    for h in heads:
        st_scr[h] = state[h]


def _mixer_b(proj, small3, conv_w3, a_log, dt_bias, o_norm_g, *, batch, seq):
    t = proj.shape[1]
    steps = seq // DN_STEP
    nch = DN_STEP // DN_CHUNK
    slab = lambda n: pl.BlockSpec((B_HEADS, DN_STEP, LANES), lambda b, s: (n, b * steps + s, 0))
    smem = pl.BlockSpec(memory_space=pltpu.SMEM)
    per_head = lambda rows, dtype: pltpu.VMEM((B_HEADS, rows, HEAD_DIM), dtype)
    return pl.pallas_call(
        functools.partial(_mixer_b_kernel, rows_per_step=DN_STEP),
        grid=(batch, steps),
        in_specs=[
            smem, smem,
            slab(0), slab(1), slab(2), slab(3),
            pl.BlockSpec((nch, SMALL_ROWS, DN_CHUNK), lambda b, s: (b * steps + s, 0, 0)),
            pl.BlockSpec((3 * B_HEADS, B_CONV, HEAD_DIM), lambda b, s: (0, 0, 0)),
            pl.BlockSpec((1, HEAD_DIM), lambda b, s: (0, 0)),
        ],
        out_specs=pl.BlockSpec((B_HEADS, DN_STEP, LANES), lambda b, s: (0, b * steps + s, 0)),
        out_shape=jax.ShapeDtypeStruct((B_HEADS, t, LANES), BF16),
        scratch_shapes=[
            pltpu.VMEM((3 * B_HEADS, DN_STEP + 8, HEAD_DIM), F32),
            per_head(DN_STEP, F32),
            per_head(DN_STEP, F32),
            per_head(DN_STEP, F32),
            pltpu.VMEM((nch * SMALL_ROWS, DN_CHUNK), F32),
            pltpu.VMEM((B_HEADS, nch, LANES), F32),
            per_head(HEAD_DIM, F32),
            per_head(DN_STEP, F32),
            per_head(DN_STEP, BF16),
            per_head(DN_STEP, BF16),
            per_head(DN_STEP, BF16),
            per_head(DN_STEP, BF16),
        ],
        compiler_params=_params("parallel", "arbitrary"),
        name="mixer_b",
    )(a_log, dt_bias, proj, proj, proj, proj, small3, conv_w3, o_norm_g)


def _mixer_c_kernel(slope_ref, q_ref, k_ref, v_ref, qg_ref, kg_ref, o_ref,
                    qs, ks, vs, ob, lb, *, seq):
    h = pl.program_id(1)
    slope = slope_ref[h]
    qs[...] = _rms(q_ref[...].astype(F32), qg_ref[...]) * (HEAD_DIM ** -0.5)
    ks[...] = _rms(k_ref[...].astype(F32), kg_ref[...])
    vs[...] = v_ref[...].astype(F32)

    qi = lax.broadcasted_iota(jnp.int32, (C_BLOCK, 2 * C_BLOCK), 0)
    kj = lax.broadcasted_iota(jnp.int32, (C_BLOCK, 2 * C_BLOCK), 1)
    delta = C_BLOCK + qi - kj

    for br, (window, dil) in enumerate(C_BRANCHES):
        span = window // dil
        seg = seq // dil
        in_band = (delta >= 0) & (delta <= span)
        bias = jnp.where(in_band, -slope * (delta * dil).astype(F32), -jnp.inf)
        bias_cur = bias[:, C_BLOCK:]

        def rows(start, size, dil=dil):
            return pl.ds(start, size, stride=dil) if dil > 1 else pl.ds(start, size)

        def attend(items, br=br):
            s = [_mm_nt(qs[q, :], ks[k, :]) + b for q, k, b in items]
            mx = [jnp.max(x, axis=-1, keepdims=True) for x in s]
            p = [jnp.exp(x - m).astype(BF16) for x, m in zip(s, mx)]
            acc = [jnp.dot(x, jnp.concatenate([vs[k, :].astype(BF16), jnp.ones((x.shape[1], LANES), BF16)], axis=1),
                           preferred_element_type=F32) for x, (_, k, _) in zip(p, items)]
            for (q, _, _), y, m in zip(items, acc, mx):
                den = y[:, HEAD_DIM:]
                ob[br, q, :] = y[:, :HEAD_DIM] / den
                lb[br, q, :] = m + jnp.log(den)

        nb = seg // C_BLOCK
        blocks = []
        for r in range(dil):
            blocks.append((rows(r, C_BLOCK), rows(r, C_BLOCK), bias_cur))
            for n in range(1, nb):
                blocks.append((rows(r + n * C_BLOCK * dil, C_BLOCK),
                               rows(r + (n - 1) * C_BLOCK * dil, 2 * C_BLOCK), bias))
        for g0 in range(0, len(blocks), C_GROUP):
            attend(blocks[g0:g0 + C_GROUP])

    def merge(n, carry):
        r = pl.ds(pl.multiple_of(n * C_BLOCK, C_BLOCK), C_BLOCK)
        l0, l1, l2 = lb[0, r, :], lb[1, r, :], lb[2, r, :]
        mx = jnp.maximum(jnp.maximum(l0, l1), l2)
        w0, w1, w2 = jnp.exp(l0 - mx), jnp.exp(l1 - mx), jnp.exp(l2 - mx)
        out = (w0 * ob[0, r, :] + w1 * ob[1, r, :] + w2 * ob[2, r, :]) / (w0 + w1 + w2)
        o_ref[r, :] = out.astype(BF16)
        return carry

    lax.fori_loop(0, seq // C_BLOCK, merge, 0)


def _mixer_c(proj, slopes, q_g, k_g, *, batch, seq):
    t = proj.shape[1]
    slab = lambda n: pl.BlockSpec((None, seq, LANES), lambda b, h: (n * C_HEADS + h, b, 0))
    vec = pl.BlockSpec((1, HEAD_DIM), lambda b, h: (0, 0))
    return pl.pallas_call(
        functools.partial(_mixer_c_kernel, seq=seq),
        grid=(batch, C_HEADS),
        in_specs=[pl.BlockSpec(memory_space=pltpu.SMEM),
                  slab(0), slab(1), slab(2), vec, vec],
        out_specs=pl.BlockSpec((None, seq, LANES), lambda b, h: (h, b, 0)),
        out_shape=jax.ShapeDtypeStruct((C_HEADS, t, LANES), BF16),
        scratch_shapes=[
            pltpu.VMEM((seq, HEAD_DIM), F32),
            pltpu.VMEM((seq, HEAD_DIM), F32),
            pltpu.VMEM((seq, HEAD_DIM), F32),
            pltpu.VMEM((len(C_BRANCHES), seq, HEAD_DIM), F32),
            pltpu.VMEM((len(C_BRANCHES), seq, LANES), F32),
        ],
        compiler_params=_params("parallel", "arbitrary"),
        name="mixer_c",
    )(slopes, proj, proj, proj, q_g, k_g)


def _out_proj_kernel(x_ref, ya_ref, yb_ref, yc_ref, w_ref, g_ref, xo_ref, ho_ref):
    mix = jnp.concatenate([ya_ref[i] for i in range(A_GROUPS)]
                          + [yb_ref[i] for i in range(B_HEADS)]
                          + [yc_ref[i] for i in range(C_HEADS)], axis=1)
    xn = x_ref[...] + jnp.dot(mix, w_ref[...], preferred_element_type=F32)
    xo_ref[...] = xn
    ho_ref[...] = _rms(xn, g_ref[...]).astype(BF16)


def _out_proj(x2, ya, yb, yc, w, gain, layer, *, tm=512):
    t = x2.shape[0]
    return pl.pallas_call(
        _out_proj_kernel,
        grid=(t // tm,),
        in_specs=[
            pl.BlockSpec((tm, D_MODEL), lambda i: (i, 0)),
            pl.BlockSpec((A_GROUPS, tm, LANES), lambda i: (0, i, 0)),
            pl.BlockSpec((B_HEADS, tm, LANES), lambda i: (0, i, 0)),
            pl.BlockSpec((C_HEADS, tm, LANES), lambda i: (0, i, 0)),
            pl.BlockSpec((None, D_MODEL, D_MODEL), lambda i: (layer, 0, 0), pipeline_mode=pl.Buffered(1)),
            pl.BlockSpec((None, 1, D_MODEL), lambda i: (layer, 0, 0)),
        ],
        out_specs=[
            pl.BlockSpec((tm, D_MODEL), lambda i: (i, 0)),
            pl.BlockSpec((tm, D_MODEL), lambda i: (i, 0)),
        ],
        out_shape=[
            jax.ShapeDtypeStruct((t, D_MODEL), F32),
            jax.ShapeDtypeStruct((t, D_MODEL), BF16),
        ],
        compiler_params=_params("parallel"),
        name="out_proj",
    )(x2, ya, yb, yc, w, gain)


def _ffn_kernel(h_ref, x_ref, wgu_ref, wd_ref, o_ref):
    @pl.when(pl.program_id(1) == 0)
    def _():
        o_ref[...] = x_ref[...]

    tf = wd_ref.shape[0]
    gu = jnp.dot(h_ref[...], wgu_ref[...], preferred_element_type=F32)
    act = (_silu(gu[:, :tf]) * gu[:, tf:]).astype(BF16)
    o_ref[...] += jnp.dot(act, wd_ref[...], preferred_element_type=F32)


def _ffn(h2, x2, w_gate_up, w_down, layer, *, tm=512):
    t = x2.shape[0]
    nf = w_gate_up.shape[1]
    tf = w_gate_up.shape[3] // 2
    return pl.pallas_call(
        _ffn_kernel,
        grid=(t // tm, nf),
        in_specs=[
            pl.BlockSpec((tm, D_MODEL), lambda i, j: (i, 0)),
            pl.BlockSpec((tm, D_MODEL), lambda i, j: (i, 0)),
            pl.BlockSpec((None, None, D_MODEL, 2 * tf), lambda i, j: (layer, j, 0, 0)),
            pl.BlockSpec((None, tf, D_MODEL), lambda i, j: (layer, j, 0)),
        ],
        out_specs=pl.BlockSpec((tm, D_MODEL), lambda i, j: (i, 0)),
        out_shape=jax.ShapeDtypeStruct((t, D_MODEL), F32),
        compiler_params=_params("parallel", "arbitrary"),
        name="ffn",
    )(h2, x2, w_gate_up, w_down)


def kernel(x, norm1_g, w_in, sgu_norm_g, w_spatial, b_spatial, conv_w, a_log, dt_bias, o_norm_g,
           q_norm_g, k_norm_g, w_out, norm2_g, w_gate_up, w_down):
    batch, seq, _ = x.shape
    depth = w_in.shape[0]
    t = batch * seq
    slopes = jnp.exp2(-8.0 * (jnp.arange(C_HEADS, dtype=F32) + 1.0) / C_HEADS)
    w_in_t = jnp.swapaxes(w_in, 1, 2).astype(BF16)
    w_c = w_in_t[:, OFF_C:, :]
    w_small = jnp.pad(w_in_t[:, OFF_BETA:OFF_C, :], ((0, 0), (0, SMALL_ROWS - 2 * B_HEADS), (0, 0)))
    w_out_b = w_out.astype(BF16)
    nf = FFN_HIDDEN // FFN_TILE
    w_gate_up_b = (w_gate_up.astype(BF16).reshape(depth, D_MODEL, 2, nf, FFN_TILE)
                   .transpose(0, 3, 1, 2, 4).reshape(depth, nf, D_MODEL, 2 * FFN_TILE))
    w_down_b = w_down.astype(BF16)
    conv_w3 = conv_w.reshape(depth, B_CONV, 3 * B_HEADS, HEAD_DIM).transpose(0, 2, 1, 3)
    norm1 = norm1_g.reshape(depth, 1, D_MODEL)
    norm2 = norm2_g.reshape(depth, 1, D_MODEL)

    x2 = x.reshape(t, D_MODEL)
    for l in range(depth):
        pa, pb, pc, small3 = _in_proj(x2, norm1, w_in_t, w_c, w_small, l)
        ya = _mixer_a(pa, sgu_norm_g[l], w_spatial[l], b_spatial[l])
        yb = _mixer_b(pb, small3, conv_w3[l], a_log[l], dt_bias[l], o_norm_g[l].reshape(1, HEAD_DIM),
                      batch=batch, seq=seq)
        yc = _mixer_c(pc, slopes, q_norm_g[l].reshape(1, HEAD_DIM), k_norm_g[l].reshape(1, HEAD_DIM),
                      batch=batch, seq=seq)
        x2, h2 = _out_proj(x2, ya, yb, yc, w_out_b, norm2, l)
        x2 = _ffn(h2, x2, w_gate_up_b, w_down_b, l)
    return x2.reshape(batch, seq, D_MODEL)
```

```python
import functools

import jax
import jax.numpy as jnp
from jax import lax
from jax.experimental import pallas as pl
from jax.experimental.pallas import tpu as pltpu

F32 = jnp.float32
BF16 = jnp.bfloat16

D_MODEL = 2048
HEAD_DIM = 128
A_GROUPS = 4
A_WIDTH = A_GROUPS * HEAD_DIM
A_CHUNK = 128
B_HEADS = 6
B_WIDTH = B_HEADS * HEAD_DIM
B_CONV = 4
C_HEADS = 6
C_WIDTH = C_HEADS * HEAD_DIM
C_BRANCHES = ((128, 1), (512, 4), (2048, 16))
C_BLOCK = 128
FFN_HIDDEN = 5632
FFN_TILE = 512
EPS = 1e-6

OFF_B = 2 * A_WIDTH
OFF_BETA = OFF_B + 4 * B_WIDTH
OFF_C = OFF_BETA + 2 * B_HEADS
IN_TOTAL = OFF_C + 3 * C_WIDTH
SLABS_A, SLABS_B, SLABS_C = 2 * A_GROUPS, 4 * B_HEADS, 3 * C_HEADS
SMALL_ROWS = 16
IN_STEPS_AB = (SLABS_A + SLABS_B) // SLABS_A
IN_STEPS_C = 3
IN_NORM_PIECES = 4

LANES = 128
DN_CHUNK = 128
DN_STEP = 512
DN_GROUP = 2
C_GROUP = 8
VMEM_LIMIT = 56 * 1024 * 1024

NT_DIMS = (((1,), (1,)), ((), ()))


def _mm(a, b):
    return jnp.dot(a.astype(BF16), b.astype(BF16), preferred_element_type=F32)


def _mm_nt(a, b):
    return lax.dot_general(a.astype(BF16), b.astype(BF16), NT_DIMS, preferred_element_type=F32)


def _sigmoid(x):
    return 0.5 * (1.0 + jnp.tanh(0.5 * x))


def _silu(x):
    return x * _sigmoid(x)


def _gelu_tanh(x):
    return 0.5 * x * (1.0 + jnp.tanh(0.7978845608028654 * (x + 0.044715 * (x * x * x))))


def _rms(x, gain):
    return x * lax.rsqrt(jnp.mean(x * x, axis=-1, keepdims=True) + EPS) * gain


def _params(*sem):
    return pltpu.CompilerParams(dimension_semantics=sem, vmem_limit_bytes=VMEM_LIMIT)


def _cast_w_in_kernel(w_ref, o_ref, *, tn):
    depth, _, d = o_ref.shape
    per_col = depth * (d // LANES)
    grp = 8 * per_col
    i = lax.broadcasted_iota(jnp.int32, (grp, grp), 0)
    c = lax.broadcasted_iota(jnp.int32, (grp, grp), 1)
    perm = jnp.where(c == (i % 8) * per_col + i // 8, 1.0, 0.0).astype(BF16)
    regrouped = [jnp.dot(perm, w_ref[g * grp:(g + 1) * grp, :].astype(BF16), preferred_element_type=F32)
                 for g in range(tn // 8)]
    for kt in range(d // LANES):
        for l in range(depth):
            r = kt * depth + l
            rows = jnp.concatenate([y[r * 8:(r + 1) * 8, :] for y in regrouped], axis=0)
            o_ref[l, :, kt * LANES:(kt + 1) * LANES] = rows.astype(BF16)


def _cast_w_in_t(w_in, *, tn=128):
    depth, d, cols = w_in.shape
    rows_view = (jnp.transpose(w_in, (2, 0, 1)).reshape(cols, depth, d // LANES, LANES)
                 .transpose(0, 2, 1, 3).reshape(cols * (d // LANES) * depth, LANES))
    per_col = depth * (d // LANES)
    return pl.pallas_call(
        functools.partial(_cast_w_in_kernel, tn=tn),
        grid=(pl.cdiv(cols, tn),),
        in_specs=[pl.BlockSpec((tn * per_col, LANES), lambda j: (j, 0))],
        out_specs=pl.BlockSpec((depth, tn, d), lambda j: (0, j, 0)),
        out_shape=jax.ShapeDtypeStruct((depth, cols, d), BF16),
        compiler_params=_params("parallel"),
        name="cast_w_in",
    )(rows_view)


def _in_proj_kernel(x_ref, g_ref, wab_ref, wc_ref, ws_ref, oa_ref, ob_ref, oc_ref, os_ref, h_scr):
    j = pl.program_id(1)

    @pl.when(j == 0)
    def _():
        piece = x_ref.shape[0] // IN_NORM_PIECES
        for r in range(IN_NORM_PIECES):
            rows = slice(r * piece, (r + 1) * piece)
            h = _rms(x_ref[rows, :], g_ref[...]).astype(BF16)
            h_scr[rows, :] = h
            res = lax.dot_general(h, wab_ref[...], NT_DIMS, preferred_element_type=F32)
            for c in range(oa_ref.shape[0]):
                oa_ref[c, rows, :] = res[:, c * LANES:(c + 1) * LANES].astype(BF16)
            small = lax.dot_general(ws_ref[...], h, NT_DIMS, preferred_element_type=F32)
            for c in range(piece // DN_CHUNK):
                os_ref[r * (piece // DN_CHUNK) + c] = small[:, c * DN_CHUNK:(c + 1) * DN_CHUNK]

    def project(w_ref, o_ref):
        res = lax.dot_general(h_scr[...], w_ref[...], NT_DIMS, preferred_element_type=F32)
        for c in range(o_ref.shape[0]):
            o_ref[c] = res[:, c * LANES:(c + 1) * LANES].astype(BF16)

    pl.when((j >= 1) & (j < IN_STEPS_AB))(lambda: project(wab_ref, ob_ref))
    pl.when(j >= IN_STEPS_AB)(lambda: project(wc_ref, oc_ref))


def _in_proj(x2, gain, w_ab, w_c, w_small, layer, *, tm=1024):
    t = x2.shape[0]
    tn_ab = SLABS_A * LANES
    tn_c = (SLABS_C // IN_STEPS_C) * LANES
    clamp = lambda v, hi: jnp.minimum(jnp.maximum(v, 0), hi)
    return pl.pallas_call(
        _in_proj_kernel,
        grid=(t // tm, IN_STEPS_AB + IN_STEPS_C),
        in_specs=[
            pl.BlockSpec((tm, D_MODEL), lambda i, j: (i, 0)),
            pl.BlockSpec((None, 1, D_MODEL), lambda i, j: (layer, 0, 0)),
            pl.BlockSpec((None, tn_ab, D_MODEL), lambda i, j: (layer, clamp(j, IN_STEPS_AB - 1), 0)),
            pl.BlockSpec((None, tn_c, D_MODEL), lambda i, j: (layer, clamp(j - IN_STEPS_AB, IN_STEPS_C - 1), 0)),
            pl.BlockSpec((None, SMALL_ROWS, D_MODEL), lambda i, j: (layer, 0, 0)),
        ],
        out_specs=[
            pl.BlockSpec((SLABS_A, tm, LANES), lambda i, j: (0, i, 0)),
            pl.BlockSpec((SLABS_A, tm, LANES), lambda i, j: (clamp(j - 1, IN_STEPS_AB - 2), i, 0)),
            pl.BlockSpec((SLABS_C // IN_STEPS_C, tm, LANES), lambda i, j: (clamp(j - IN_STEPS_AB, IN_STEPS_C - 1), i, 0)),
            pl.BlockSpec((tm // DN_CHUNK, SMALL_ROWS, DN_CHUNK), lambda i, j: (i, 0, 0)),
        ],
        out_shape=[
            jax.ShapeDtypeStruct((SLABS_A, t, LANES), BF16),
            jax.ShapeDtypeStruct((SLABS_B, t, LANES), BF16),
            jax.ShapeDtypeStruct((SLABS_C, t, LANES), BF16),
            jax.ShapeDtypeStruct((t // DN_CHUNK, SMALL_ROWS, DN_CHUNK), F32),
        ],
        scratch_shapes=[pltpu.VMEM((tm, D_MODEL), BF16)],
        compiler_params=_params("parallel", "arbitrary"),
        name="in_proj",
    )(x2, gain, w_ab, w_c, w_small)


def _mixer_a_kernel(p_ref, sg_ref, ws_ref, bs_ref, o_ref, *, chunks):
    ii = lax.broadcasted_iota(jnp.int32, (A_CHUNK, A_CHUNK), 0)
    jj = lax.broadcasted_iota(jnp.int32, (A_CHUNK, A_CHUNK), 1)
    for g in range(A_GROUPS):
        w = jnp.where(ii >= jj, ws_ref[g], 0.0).astype(BF16)
        gain = sg_ref[g:g + 1, :]
        bias = bs_ref[g]
        for c in range(chunks):
            rows = slice(c * A_CHUNK, (c + 1) * A_CHUNK)
            u = _gelu_tanh(p_ref[g, rows, :].astype(F32))
            v = _rms(_gelu_tanh(p_ref[A_GROUPS + g, rows, :].astype(F32)), gain)
            z = jnp.dot(w, v.astype(BF16), preferred_element_type=F32) + bias
            o_ref[g, rows, :] = (u * z).astype(BF16)


def _mixer_a(proj, sgu_g, w_s, b_s, *, tm=512):
    t = proj.shape[1]
    return pl.pallas_call(
        functools.partial(_mixer_a_kernel, chunks=tm // A_CHUNK),
        grid=(t // tm,),
        in_specs=[
            pl.BlockSpec((SLABS_A, tm, LANES), lambda i: (0, i, 0)),
            pl.BlockSpec((A_GROUPS, HEAD_DIM), lambda i: (0, 0)),
            pl.BlockSpec((A_GROUPS, A_CHUNK, A_CHUNK), lambda i: (0, 0, 0)),
            pl.BlockSpec((A_GROUPS, A_CHUNK, 1), lambda i: (0, 0, 0)),
        ],
        out_specs=pl.BlockSpec((A_GROUPS, tm, LANES), lambda i: (0, i, 0)),
        out_shape=jax.ShapeDtypeStruct((A_GROUPS, t, LANES), BF16),
        compiler_params=_params("parallel"),
        name="mixer_a",
    )(proj, sgu_g, w_s, b_s.reshape(A_GROUPS, A_CHUNK, 1))


def _unit_lower_inverse(a_list, ii, jj):
    def same_block(shift):
        return (ii >> shift) == (jj >> shift)

    a_pow = [jnp.where(same_block(4), a, 0.0) for a in a_list]
    eye = jnp.where(ii == jj, 1.0, 0.0)
    p = [eye - x for x in a_pow]
    for _ in range(3):
        a_pow = [_mm(x, x) for x in a_pow]
        p = [y + _mm(y, x) for y, x in zip(p, a_pow)]
    for shift in (4, 5, 6):
        off = same_block(shift + 1) & jnp.logical_not(same_block(shift))
        t = [_mm(jnp.where(off, a, 0.0), y) for a, y in zip(a_list, p)]
        p = [y - _mm(y, x) for y, x in zip(p, t)]
    return p


def _mixer_b_kernel(alog_ref, dtb_ref, q_ref, k_ref, v_ref, gate_ref, small_ref, cw_ref, og_ref, o_ref,
                    xpad, qn, kn, vn, gb_scr, el_scr, st_scr, u_scr, w_scr, at_scr, qg_scr, kdt_scr, *, rows_per_step):
    nchunks = rows_per_step // DN_CHUNK
    pad = 8
    sq = (DN_CHUNK, DN_CHUNK)

    @pl.when(pl.program_id(1) == 0)
    def _():
        xpad[:, 0:pad, :] = jnp.zeros((3 * B_HEADS, pad, HEAD_DIM), F32)
        st_scr[...] = jnp.zeros_like(st_scr)

    tensors = ((q_ref, qn), (k_ref, kn), (v_ref, vn))
    for kind, (src, _) in enumerate(tensors):
        for h in range(B_HEADS):
            xpad[kind * B_HEADS + h, pad:pad + rows_per_step, :] = src[h].astype(F32)

    def conv_chunks(chunks):
        for kind, (_, dst) in enumerate(tensors):
            for h in range(B_HEADS):
                idx = kind * B_HEADS + h
                cw = cw_ref[idx]
                for t in chunks:
                    base = pad + t * DN_CHUNK
                    y = cw[B_CONV - 1:B_CONV, :] * xpad[idx, base:base + DN_CHUNK, :]
                    for s in range(1, B_CONV):
                        y = y + cw[B_CONV - 1 - s:B_CONV - s, :] * xpad[idx, base - s:base - s + DN_CHUNK, :]
                    y = _silu(y)
                    if kind < 2:
                        y = y * lax.rsqrt(jnp.sum(y * y, axis=-1, keepdims=True) + EPS)
                    if kind == 0:
                        y = y * (HEAD_DIM ** -0.5)
                    dst[h, t * DN_CHUNK:(t + 1) * DN_CHUNK, :] = y

    small = small_ref[...].reshape(nchunks * SMALL_ROWS, DN_CHUNK)
    j = lax.broadcasted_iota(jnp.int32, small.shape, 0) % SMALL_ROWS
    lane = lax.broadcasted_iota(jnp.int32, small.shape, 1)
    dtb = jnp.zeros_like(small)
    alog = jnp.zeros_like(small)
    for h in range(B_HEADS):
        dtb = jnp.where(j == B_HEADS + h, dtb_ref[h], dtb)
        alog = jnp.where(j == B_HEADS + h, alog_ref[h], alog)
    xa = small + dtb
    g = -jnp.exp(alog) * (jnp.maximum(xa, 0.0) + jnp.log(1.0 + jnp.exp(-jnp.abs(xa))))
    shift = 1
    while shift < DN_CHUNK:
        g = g + jnp.where(lane >= shift, pltpu.roll(g, shift, 1), 0.0)
        shift *= 2
    gb_scr[...] = jnp.where(j < B_HEADS, _sigmoid(small), g)

    ii = lax.broadcasted_iota(jnp.int32, sq, 0)
    jj = lax.broadcasted_iota(jnp.int32, sq, 1)

    conv_chunks(range(DN_GROUP))
    for c0 in range(0, nchunks, DN_GROUP):
        items = [(c, h) for c in range(c0, c0 + DN_GROUP) for h in range(B_HEADS)]
        rows = {c: slice(c * DN_CHUNK, (c + 1) * DN_CHUNK) for c, _ in items}
        g_j = [jnp.broadcast_to(gb_scr[c * SMALL_ROWS + B_HEADS + h:c * SMALL_ROWS + B_HEADS + h + 1, :], sq)
               for c, h in items]
        b_i = [jnp.broadcast_to(gb_scr[c * SMALL_ROWS + h:c * SMALL_ROWS + h + 1, :], sq).T
               for c, h in items]
        g_i = [x.T for x in g_j]
        decay = [jnp.exp(jnp.where(ii >= jj, x - y, -jnp.inf)) for x, y in zip(g_i, g_j)]
        kb = [kn[h, rows[c], :] * b for (c, h), b in zip(items, b_i)]
        a = [jnp.where(ii > jj, _mm_nt(x, kn[h, rows[c], :]) * d, 0.0) for x, (c, h), d in zip(kb, items, decay)]
        t_inv = _unit_lower_inverse(a, ii, jj)
        e_i = [jnp.exp(x) for x in g_i]
        sol = [_mm(t, jnp.concatenate([vn[h, rows[c], :] * b, x * e], axis=1))
               for t, (c, h), b, x, e in zip(t_inv, items, b_i, kb, e_i)]
        attn = [_mm_nt(qn[h, rows[c], :], kn[h, rows[c], :]) * d for (c, h), d in zip(items, decay)]
        for i, (c, h) in enumerate(items):
            r = rows[c]
            g_last = jnp.broadcast_to(g_j[i][:, DN_CHUNK - 1:DN_CHUNK], sq)
            u_scr[h, r, :] = sol[i][:, :HEAD_DIM]
            w_scr[h, r, :] = sol[i][:, HEAD_DIM:].astype(BF16)
            at_scr[h, r, :] = attn[i].astype(BF16)
            qg_scr[h, r, :] = (qn[h, r, :] * e_i[i]).astype(BF16)
            kdt_scr[h, r, :] = (kn[h, r, :] * jnp.exp(g_last - g_i[i])).T.astype(BF16)
            el_scr[h, c:c + 1, :] = jnp.exp(g_last[0:1, :])
        if c0 + DN_GROUP < nchunks:
            conv_chunks(range(c0 + DN_GROUP, c0 + 2 * DN_GROUP))
    for idx in range(3 * B_HEADS):
        xpad[idx, 0:pad, :] = xpad[idx, rows_per_step:rows_per_step + pad, :]

    heads = range(B_HEADS)
    state = [st_scr[h] for h in heads]
    for c in range(nchunks):
        r = slice(c * DN_CHUNK, (c + 1) * DN_CHUNK)
        sb = [s.astype(BF16) for s in state]
        v_new = [u_scr[h, r, :] - jnp.dot(w_scr[h, r, :], sb[h], preferred_element_type=F32) for h in heads]
        vb = [x.astype(BF16) for x in v_new]
        o = [jnp.dot(qg_scr[h, r, :], sb[h], preferred_element_type=F32)
             + jnp.dot(at_scr[h, r, :], vb[h], preferred_element_type=F32) for h in heads]
        state = [state[h] * el_scr[h, c:c + 1, :] + jnp.dot(kdt_scr[h, r, :], vb[h], preferred_element_type=F32)
                 for h in heads]
        for h in heads:
            y = _rms(o[h], og_ref[...]) * _silu(gate_ref[h, r, :].astype(F32))
            o_ref[h, r, :] = y.astype(BF16)
    for h in heads:
        st_scr[h] = state[h]


def _mixer_b(proj, small3, conv_w3, a_log, dt_bias, o_norm_g, *, batch, seq):
    t = proj.shape[1]
    steps = seq // DN_STEP
    nch = DN_STEP // DN_CHUNK
    slab = lambda n: pl.BlockSpec((B_HEADS, DN_STEP, LANES), lambda b, s: (n, b * steps + s, 0))
    smem = pl.BlockSpec(memory_space=pltpu.SMEM)
    per_head = lambda rows, dtype: pltpu.VMEM((B_HEADS, rows, HEAD_DIM), dtype)
    return pl.pallas_call(
        functools.partial(_mixer_b_kernel, rows_per_step=DN_STEP),
        grid=(batch, steps),
        in_specs=[
            smem, smem,
            slab(0), slab(1), slab(2), slab(3),
            pl.BlockSpec((nch, SMALL_ROWS, DN_CHUNK), lambda b, s: (b * steps + s, 0, 0)),
            pl.BlockSpec((3 * B_HEADS, B_CONV, HEAD_DIM), lambda b, s: (0, 0, 0)),
            pl.BlockSpec((1, HEAD_DIM), lambda b, s: (0, 0)),
        ],
        out_specs=pl.BlockSpec((B_HEADS, DN_STEP, LANES), lambda b, s: (0, b * steps + s, 0)),
        out_shape=jax.ShapeDtypeStruct((B_HEADS, t, LANES), BF16),
        scratch_shapes=[
            pltpu.VMEM((3 * B_HEADS, DN_STEP + 8, HEAD_DIM), F32),
            per_head(DN_STEP, F32),
            per_head(DN_STEP, F32),
            per_head(DN_STEP, F32),
            pltpu.VMEM((nch * SMALL_ROWS, DN_CHUNK), F32),
            pltpu.VMEM((B_HEADS, nch, LANES), F32),
            per_head(HEAD_DIM, F32),
            per_head(DN_STEP, F32),
            per_head(DN_STEP, BF16),
            per_head(DN_STEP, BF16),
            per_head(DN_STEP, BF16),
            per_head(DN_STEP, BF16),
        ],
        compiler_params=_params("parallel", "arbitrary"),
        name="mixer_b",
    )(a_log, dt_bias, proj, proj, proj, proj, small3, conv_w3, o_norm_g)


def _mixer_c_kernel(slope_ref, q_ref, k_ref, v_ref, qg_ref, kg_ref, o_ref,
                    qs, ks, vs, ob, lb, *, seq):
    h = pl.program_id(1)
    slope = slope_ref[h]
    qs[...] = _rms(q_ref[...].astype(F32), qg_ref[...]) * (HEAD_DIM ** -0.5)
    ks[...] = _rms(k_ref[...].astype(F32), kg_ref[...])
    vs[...] = v_ref[...].astype(F32)

    qi = lax.broadcasted_iota(jnp.int32, (C_BLOCK, 2 * C_BLOCK), 0)
    kj = lax.broadcasted_iota(jnp.int32, (C_BLOCK, 2 * C_BLOCK), 1)
    delta = C_BLOCK + qi - kj

    for br, (window, dil) in enumerate(C_BRANCHES):
        span = window // dil
        seg = seq // dil
        in_band = (delta >= 0) & (delta <= span)
        bias = jnp.where(in_band, -slope * (delta * dil).astype(F32), -jnp.inf)
        bias_cur = bias[:, C_BLOCK:]

        def rows(start, size, dil=dil):
            return pl.ds(start, size, stride=dil) if dil > 1 else pl.ds(start, size)

        def attend(items, br=br):
            s = [_mm_nt(qs[q, :], ks[k, :]) + b for q, k, b in items]
            mx = [jnp.max(x, axis=-1, keepdims=True) for x in s]
            p = [jnp.exp(x - m).astype(BF16) for x, m in zip(s, mx)]
            acc = [jnp.dot(x, jnp.concatenate([vs[k, :].astype(BF16), jnp.ones((x.shape[1], LANES), BF16)], axis=1),
                           preferred_element_type=F32) for x, (_, k, _) in zip(p, items)]
            for (q, _, _), y, m in zip(items, acc, mx):
                den = y[:, HEAD_DIM:]
                ob[br, q, :] = y[:, :HEAD_DIM] / den
                lb[br, q, :] = m + jnp.log(den)

        nb = seg // C_BLOCK
        blocks = []
        for r in range(dil):
            blocks.append((rows(r, C_BLOCK), rows(r, C_BLOCK), bias_cur))
            for n in range(1, nb):
                blocks.append((rows(r + n * C_BLOCK * dil, C_BLOCK),
                               rows(r + (n - 1) * C_BLOCK * dil, 2 * C_BLOCK), bias))
        for g0 in range(0, len(blocks), C_GROUP):
            attend(blocks[g0:g0 + C_GROUP])

    def merge(n, carry):
        r = pl.ds(pl.multiple_of(n * C_BLOCK, C_BLOCK), C_BLOCK)
        l0, l1, l2 = lb[0, r, :], lb[1, r, :], lb[2, r, :]
        mx = jnp.maximum(jnp.maximum(l0, l1), l2)
        w0, w1, w2 = jnp.exp(l0 - mx), jnp.exp(l1 - mx), jnp.exp(l2 - mx)
        out = (w0 * ob[0, r, :] + w1 * ob[1, r, :] + w2 * ob[2, r, :]) / (w0 + w1 + w2)
        o_ref[r, :] = out.astype(BF16)
        return carry

    lax.fori_loop(0, seq // C_BLOCK, merge, 0)


def _mixer_c(proj, slopes, q_g, k_g, *, batch, seq):
    t = proj.shape[1]
    slab = lambda n: pl.BlockSpec((None, seq, LANES), lambda b, h: (n * C_HEADS + h, b, 0))
    vec = pl.BlockSpec((1, HEAD_DIM), lambda b, h: (0, 0))
    return pl.pallas_call(
        functools.partial(_mixer_c_kernel, seq=seq),
        grid=(batch, C_HEADS),
        in_specs=[pl.BlockSpec(memory_space=pltpu.SMEM),
                  slab(0), slab(1), slab(2), vec, vec],
        out_specs=pl.BlockSpec((None, seq, LANES), lambda b, h: (h, b, 0)),
        out_shape=jax.ShapeDtypeStruct((C_HEADS, t, LANES), BF16),
        scratch_shapes=[
            pltpu.VMEM((seq, HEAD_DIM), F32),
            pltpu.VMEM((seq, HEAD_DIM), F32),
            pltpu.VMEM((seq, HEAD_DIM), F32),
            pltpu.VMEM((len(C_BRANCHES), seq, HEAD_DIM), F32),
            pltpu.VMEM((len(C_BRANCHES), seq, LANES), F32),
        ],
        compiler_params=_params("parallel", "arbitrary"),
        name="mixer_c",
    )(slopes, proj, proj, proj, q_g, k_g)


def _out_proj_kernel(x_ref, ya_ref, yb_ref, yc_ref, w_ref, g_ref, xo_ref, ho_ref):
    mix = jnp.concatenate([ya_ref[i] for i in range(A_GROUPS)]
                          + [yb_ref[i] for i in range(B_HEADS)]
                          + [yc_ref[i] for i in range(C_HEADS)], axis=1)
    xn = x_ref[...] + jnp.dot(mix, w_ref[...], preferred_element_type=F32)
    xo_ref[...] = xn
    ho_ref[...] = _rms(xn, g_ref[...]).astype(BF16)


def _out_proj(x2, ya, yb, yc, w, gain, layer, *, tm=512):
    t = x2.shape[0]
    return pl.pallas_call(
        _out_proj_kernel,
        grid=(t // tm,),
        in_specs=[
            pl.BlockSpec((tm, D_MODEL), lambda i: (i, 0)),
            pl.BlockSpec((A_GROUPS, tm, LANES), lambda i: (0, i, 0)),
            pl.BlockSpec((B_HEADS, tm, LANES), lambda i: (0, i, 0)),
            pl.BlockSpec((C_HEADS, tm, LANES), lambda i: (0, i, 0)),
            pl.BlockSpec((None, D_MODEL, D_MODEL), lambda i: (layer, 0, 0), pipeline_mode=pl.Buffered(1)),
            pl.BlockSpec((None, 1, D_MODEL), lambda i: (layer, 0, 0)),
        ],
        out_specs=[
            pl.BlockSpec((tm, D_MODEL), lambda i: (i, 0)),
            pl.BlockSpec((tm, D_MODEL), lambda i: (i, 0)),
        ],
        out_shape=[
            jax.ShapeDtypeStruct((t, D_MODEL), F32),
            jax.ShapeDtypeStruct((t, D_MODEL), BF16),
        ],
        compiler_params=_params("parallel"),
        name="out_proj",
    )(x2, ya, yb, yc, w, gain)


def _ffn_kernel(h_ref, x_ref, wg_ref, wu_ref, wd_ref, o_ref):
    @pl.when(pl.program_id(1) == 0)
    def _():
        o_ref[...] = x_ref[...]

    h = h_ref[...]
    gt = jnp.dot(h, wg_ref[...], preferred_element_type=F32)
    up = jnp.dot(h, wu_ref[...], preferred_element_type=F32)
    act = (_silu(gt) * up).astype(BF16)
    o_ref[...] += jnp.dot(act, wd_ref[...], preferred_element_type=F32)


def _ffn(h2, x2, w_gate_up, w_down, layer, *, tm=512, tf=FFN_TILE):
    t = x2.shape[0]
    nf = FFN_HIDDEN // tf
    return pl.pallas_call(
        _ffn_kernel,
        grid=(t // tm, nf),
        in_specs=[
            pl.BlockSpec((tm, D_MODEL), lambda i, j: (i, 0)),
            pl.BlockSpec((tm, D_MODEL), lambda i, j: (i, 0)),
            pl.BlockSpec((None, D_MODEL, tf), lambda i, j: (layer, 0, j)),
            pl.BlockSpec((None, D_MODEL, tf), lambda i, j: (layer, 0, nf + j)),
            pl.BlockSpec((None, tf, D_MODEL), lambda i, j: (layer, j, 0)),
        ],
        out_specs=pl.BlockSpec((tm, D_MODEL), lambda i, j: (i, 0)),
        out_shape=jax.ShapeDtypeStruct((t, D_MODEL), F32),
        compiler_params=_params("parallel", "arbitrary"),
        name="ffn",
    )(h2, x2, w_gate_up, w_gate_up, w_down)


def kernel(x, norm1_g, w_in, sgu_norm_g, w_spatial, b_spatial, conv_w, a_log, dt_bias, o_norm_g,
           q_norm_g, k_norm_g, w_out, norm2_g, w_gate_up, w_down):
    batch, seq, _ = x.shape
    depth = w_in.shape[0]
    t = batch * seq
    slopes = jnp.exp2(-8.0 * (jnp.arange(C_HEADS, dtype=F32) + 1.0) / C_HEADS)
    w_in_t = _cast_w_in_t(w_in)
    w_c = w_in_t[:, OFF_C:, :]
    w_small = jnp.pad(w_in_t[:, OFF_BETA:OFF_C, :], ((0, 0), (0, SMALL_ROWS - 2 * B_HEADS), (0, 0)))
    w_out_b = w_out.astype(BF16)
    w_gate_up_b = w_gate_up.astype(BF16)
    w_down_b = w_down.astype(BF16)
    conv_w3 = conv_w.reshape(depth, B_CONV, 3 * B_HEADS, HEAD_DIM).transpose(0, 2, 1, 3)
    norm1 = norm1_g.reshape(depth, 1, D_MODEL)
    norm2 = norm2_g.reshape(depth, 1, D_MODEL)

    x2 = x.reshape(t, D_MODEL)
    for l in range(depth):
        pa, pb, pc, small3 = _in_proj(x2, norm1, w_in_t, w_c, w_small, l)
        ya = _mixer_a(pa, sgu_norm_g[l], w_spatial[l], b_spatial[l])
        yb = _mixer_b(pb, small3, conv_w3[l], a_log[l], dt_bias[l], o_norm_g[l].reshape(1, HEAD_DIM),
                      batch=batch, seq=seq)
        yc = _mixer_c(pc, slopes, q_norm_g[l].reshape(1, HEAD_DIM), k_norm_g[l].reshape(1, HEAD_DIM),
                      batch=batch, seq=seq)
        x2, h2 = _out_proj(x2, ya, yb, yc, w_out_b, norm2, l)
        x2 = _ffn(h2, x2, w_gate_up_b, w_down_b, l)
    return x2.reshape(batch, seq, D_MODEL)
```

```python
import functools

import jax
import jax.numpy as jnp
from jax import lax
from jax.experimental import pallas as pl
from jax.experimental.pallas import tpu as pltpu

F32 = jnp.float32
BF16 = jnp.bfloat16

D_MODEL = 2048
HEAD_DIM = 128
A_GROUPS = 4
A_WIDTH = A_GROUPS * HEAD_DIM
A_CHUNK = 128
B_HEADS = 6
B_WIDTH = B_HEADS * HEAD_DIM
B_CONV = 4
C_HEADS = 6
C_WIDTH = C_HEADS * HEAD_DIM
C_BRANCHES = ((128, 1), (512, 4), (2048, 16))
C_BLOCK = 128
FFN_HIDDEN = 5632
FFN_TILE = 512
EPS = 1e-6

OFF_B = 2 * A_WIDTH
OFF_BETA = OFF_B + 4 * B_WIDTH
OFF_C = OFF_BETA + 2 * B_HEADS
IN_TOTAL = OFF_C + 3 * C_WIDTH
SLABS_A, SLABS_B, SLABS_C = 2 * A_GROUPS, 4 * B_HEADS, 3 * C_HEADS
SMALL_ROWS = 16
IN_STEPS_AB = (SLABS_A + SLABS_B) // SLABS_A
IN_STEPS_C = 3
IN_NORM_PIECES = 4

LANES = 128
DN_CHUNK = 128
DN_STEP = 512
DN_GROUP = 2
C_GROUP = 8
VMEM_LIMIT = 56 * 1024 * 1024

NT_DIMS = (((1,), (1,)), ((), ()))


def _mm(a, b):
    return jnp.dot(a.astype(BF16), b.astype(BF16), preferred_element_type=F32)


def _mm_nt(a, b):
    return lax.dot_general(a.astype(BF16), b.astype(BF16), NT_DIMS, preferred_element_type=F32)


def _sigmoid(x):
    return 0.5 * (1.0 + jnp.tanh(0.5 * x))


def _silu(x):
    return x * _sigmoid(x)


def _gelu_tanh(x):
    return 0.5 * x * (1.0 + jnp.tanh(0.7978845608028654 * (x + 0.044715 * (x * x * x))))


def _rms(x, gain):
    return x * lax.rsqrt(jnp.mean(x * x, axis=-1, keepdims=True) + EPS) * gain


def _params(*sem):
    return pltpu.CompilerParams(dimension_semantics=sem, vmem_limit_bytes=VMEM_LIMIT)


def _cast_w_in_kernel(w_ref, o_ref, *, tn, total_rows):
    depth, _, d = o_ref.shape
    per_col = depth * (d // LANES)
    grp = 8 * per_col
    i = lax.broadcasted_iota(jnp.int32, (grp, grp), 0)
    c = lax.broadcasted_iota(jnp.int32, (grp, grp), 1)
    perm = jnp.where(c == (i % 8) * per_col + i // 8, 1.0, 0.0).astype(BF16)
    valid = total_rows - pl.program_id(0) * (tn * per_col)
    row = lax.broadcasted_iota(jnp.int32, (grp, LANES), 0)
    regrouped = [jnp.dot(perm, jnp.where(row + g * grp < valid, w_ref[g * grp:(g + 1) * grp, :], 0.0).astype(BF16),
                         preferred_element_type=F32)
                 for g in range(tn // 8)]
    for kt in range(d // LANES):
        for l in range(depth):
            r = kt * depth + l
            rows = jnp.concatenate([y[r * 8:(r + 1) * 8, :] for y in regrouped], axis=0)
            o_ref[l, :, kt * LANES:(kt + 1) * LANES] = rows.astype(BF16)


def _cast_w_in_t(w_in, *, tn=128):
    depth, d, cols = w_in.shape
    rows_view = (jnp.transpose(w_in, (2, 0, 1)).reshape(cols, depth, d // LANES, LANES)
                 .transpose(0, 2, 1, 3).reshape(cols * (d // LANES) * depth, LANES))
    per_col = depth * (d // LANES)
    return pl.pallas_call(
        functools.partial(_cast_w_in_kernel, tn=tn, total_rows=cols * per_col),
        grid=(pl.cdiv(cols, tn),),
        in_specs=[pl.BlockSpec((tn * per_col, LANES), lambda j: (j, 0))],
        out_specs=pl.BlockSpec((depth, tn, d), lambda j: (0, j, 0)),
        out_shape=jax.ShapeDtypeStruct((depth, cols, d), BF16),
        compiler_params=_params("parallel"),
        name="cast_w_in",
    )(rows_view)


def _in_proj_kernel(x_ref, g_ref, wab_ref, wc_ref, ws_ref, oa_ref, ob_ref, oc_ref, os_ref, h_scr):
    j = pl.program_id(1)

    @pl.when(j == 0)
    def _():
        piece = x_ref.shape[0] // IN_NORM_PIECES
        for r in range(IN_NORM_PIECES):
            rows = slice(r * piece, (r + 1) * piece)
            h = _rms(x_ref[rows, :], g_ref[...]).astype(BF16)
            h_scr[rows, :] = h
            res = lax.dot_general(h, wab_ref[...], NT_DIMS, preferred_element_type=F32)
            for c in range(oa_ref.shape[0]):
                oa_ref[c, rows, :] = res[:, c * LANES:(c + 1) * LANES].astype(BF16)
            small = lax.dot_general(ws_ref[...], h, NT_DIMS, preferred_element_type=F32)
            for c in range(piece // DN_CHUNK):
                os_ref[r * (piece // DN_CHUNK) + c] = small[:, c * DN_CHUNK:(c + 1) * DN_CHUNK]

    def project(w_ref, o_ref):
        res = lax.dot_general(h_scr[...], w_ref[...], NT_DIMS, preferred_element_type=F32)
        for c in range(o_ref.shape[0]):
            o_ref[c] = res[:, c * LANES:(c + 1) * LANES].astype(BF16)

    pl.when((j >= 1) & (j <= IN_STEPS_C))(lambda: project(wc_ref, oc_ref))
    pl.when(j > IN_STEPS_C)(lambda: project(wab_ref, ob_ref))


def _in_proj(x2, gain, w_ab, w_c, w_small, layer, *, tm=1024):
    t = x2.shape[0]
    tn_ab = SLABS_A * LANES
    tn_c = (SLABS_C // IN_STEPS_C) * LANES
    clamp = lambda v, hi: jnp.minimum(jnp.maximum(v, 0), hi)
    return pl.pallas_call(
        _in_proj_kernel,
        grid=(t // tm, IN_STEPS_AB + IN_STEPS_C),
        in_specs=[
            pl.BlockSpec((tm, D_MODEL), lambda i, j: (i, 0)),
            pl.BlockSpec((None, 1, D_MODEL), lambda i, j: (layer, 0, 0)),
            pl.BlockSpec((None, tn_ab, D_MODEL), lambda i, j: (layer, clamp(j - IN_STEPS_C, IN_STEPS_AB - 1), 0)),
            pl.BlockSpec((None, tn_c, D_MODEL), lambda i, j: (layer, clamp(j - 1, IN_STEPS_C - 1), 0)),
            pl.BlockSpec((None, SMALL_ROWS, D_MODEL), lambda i, j: (layer, 0, 0)),
        ],
        out_specs=[
            pl.BlockSpec((SLABS_A, tm, LANES), lambda i, j: (0, i, 0)),
            pl.BlockSpec((SLABS_A, tm, LANES), lambda i, j: (clamp(j - IN_STEPS_C - 1, IN_STEPS_AB - 2), i, 0)),
            pl.BlockSpec((SLABS_C // IN_STEPS_C, tm, LANES), lambda i, j: (clamp(j - 1, IN_STEPS_C - 1), i, 0)),
            pl.BlockSpec((tm // DN_CHUNK, SMALL_ROWS, DN_CHUNK), lambda i, j: (i, 0, 0)),
        ],
        out_shape=[
            jax.ShapeDtypeStruct((SLABS_A, t, LANES), BF16),
            jax.ShapeDtypeStruct((SLABS_B, t, LANES), BF16),
            jax.ShapeDtypeStruct((SLABS_C, t, LANES), BF16),
            jax.ShapeDtypeStruct((t // DN_CHUNK, SMALL_ROWS, DN_CHUNK), F32),
        ],
        scratch_shapes=[pltpu.VMEM((tm, D_MODEL), BF16)],
        compiler_params=_params("parallel", "arbitrary"),
        name="in_proj",
    )(x2, gain, w_ab, w_c, w_small)


def _mixer_a_kernel(p_ref, sg_ref, ws_ref, bs_ref, o_ref, *, chunks):
    ii = lax.broadcasted_iota(jnp.int32, (A_CHUNK, A_CHUNK), 0)
    jj = lax.broadcasted_iota(jnp.int32, (A_CHUNK, A_CHUNK), 1)
    for g in range(A_GROUPS):
        w = jnp.where(ii >= jj, ws_ref[g], 0.0).astype(BF16)
        gain = sg_ref[g:g + 1, :]
        bias = bs_ref[g]
        for c in range(chunks):
            rows = slice(c * A_CHUNK, (c + 1) * A_CHUNK)
            u = _gelu_tanh(p_ref[g, rows, :].astype(F32))
            v = _rms(_gelu_tanh(p_ref[A_GROUPS + g, rows, :].astype(F32)), gain)
            z = jnp.dot(w, v.astype(BF16), preferred_element_type=F32) + bias
            o_ref[g, rows, :] = (u * z).astype(BF16)


def _mixer_a(proj, sgu_g, w_s, b_s, *, tm=512):
    t = proj.shape[1]
    return pl.pallas_call(
        functools.partial(_mixer_a_kernel, chunks=tm // A_CHUNK),
        grid=(t // tm,),
        in_specs=[
            pl.BlockSpec((SLABS_A, tm, LANES), lambda i: (0, i, 0)),
            pl.BlockSpec((A_GROUPS, HEAD_DIM), lambda i: (0, 0)),
            pl.BlockSpec((A_GROUPS, A_CHUNK, A_CHUNK), lambda i: (0, 0, 0)),
            pl.BlockSpec((A_GROUPS, A_CHUNK, 1), lambda i: (0, 0, 0)),
        ],
        out_specs=pl.BlockSpec((A_GROUPS, tm, LANES), lambda i: (0, i, 0)),
        out_shape=jax.ShapeDtypeStruct((A_GROUPS, t, LANES), BF16),
        compiler_params=_params("parallel"),
        name="mixer_a",
    )(proj, sgu_g, w_s, b_s.reshape(A_GROUPS, A_CHUNK, 1))


def _unit_lower_inverse(a_list, ii, jj):
    def same_block(shift):
        return (ii >> shift) == (jj >> shift)

    a_pow = [jnp.where(same_block(4), a, 0.0) for a in a_list]
    eye = jnp.where(ii == jj, 1.0, 0.0)
    p = [eye - x for x in a_pow]
    for _ in range(3):
        a_pow = [_mm(x, x) for x in a_pow]
        p = [y + _mm(y, x) for y, x in zip(p, a_pow)]
    for shift in (4, 5, 6):
        off = same_block(shift + 1) & jnp.logical_not(same_block(shift))
        t = [_mm(jnp.where(off, a, 0.0), y) for a, y in zip(a_list, p)]
        p = [y - _mm(y, x) for y, x in zip(p, t)]
    return p


def _mixer_b_kernel(alog_ref, dtb_ref, q_ref, k_ref, v_ref, gate_ref, small_ref, cw_ref, og_ref, o_ref,
                    xpad, qn, kn, vn, gb_scr, el_scr, st_scr, u_scr, w_scr, at_scr, qg_scr, kdt_scr, *, rows_per_step):
    nchunks = rows_per_step // DN_CHUNK
    pad = 8
    sq = (DN_CHUNK, DN_CHUNK)

    @pl.when(pl.program_id(1) == 0)
    def _():
        xpad[:, 0:pad, :] = jnp.zeros((3 * B_HEADS, pad, HEAD_DIM), F32)
        st_scr[...] = jnp.zeros_like(st_scr)

    tensors = ((q_ref, qn), (k_ref, kn), (v_ref, vn))
    for kind, (src, _) in enumerate(tensors):
        for h in range(B_HEADS):
            xpad[kind * B_HEADS + h, pad:pad + rows_per_step, :] = src[h].astype(F32)

    def conv_chunks(chunks):
        for kind, (_, dst) in enumerate(tensors):
            for h in range(B_HEADS):
                idx = kind * B_HEADS + h
                cw = cw_ref[idx]
                for t in chunks:
                    base = pad + t * DN_CHUNK
                    y = cw[B_CONV - 1:B_CONV, :] * xpad[idx, base:base + DN_CHUNK, :]
                    for s in range(1, B_CONV):
                        y = y + cw[B_CONV - 1 - s:B_CONV - s, :] * xpad[idx, base - s:base - s + DN_CHUNK, :]
                    y = _silu(y)
                    if kind < 2:
                        y = y * lax.rsqrt(jnp.sum(y * y, axis=-1, keepdims=True) + EPS)
                    if kind == 0:
                        y = y * (HEAD_DIM ** -0.5)
                    dst[h, t * DN_CHUNK:(t + 1) * DN_CHUNK, :] = y

    small = small_ref[...].reshape(nchunks * SMALL_ROWS, DN_CHUNK)
    j = lax.broadcasted_iota(jnp.int32, small.shape, 0) % SMALL_ROWS
    lane = lax.broadcasted_iota(jnp.int32, small.shape, 1)
    dtb = jnp.zeros_like(small)
    alog = jnp.zeros_like(small)
    for h in range(B_HEADS):
        dtb = jnp.where(j == B_HEADS + h, dtb_ref[h], dtb)
        alog = jnp.where(j == B_HEADS + h, alog_ref[h], alog)
    xa = small + dtb
    g = -jnp.exp(alog) * (jnp.maximum(xa, 0.0) + jnp.log(1.0 + jnp.exp(-jnp.abs(xa))))
    shift = 1
    while shift < DN_CHUNK:
        g = g + jnp.where(lane >= shift, pltpu.roll(g, shift, 1), 0.0)
        shift *= 2
    gb_scr[...] = jnp.where(j < B_HEADS, _sigmoid(small), g)

    ii = lax.broadcasted_iota(jnp.int32, sq, 0)
    jj = lax.broadcasted_iota(jnp.int32, sq, 1)

    conv_chunks(range(DN_GROUP))
    for c0 in range(0, nchunks, DN_GROUP):
        items = [(c, h) for c in range(c0, c0 + DN_GROUP) for h in range(B_HEADS)]
        rows = {c: slice(c * DN_CHUNK, (c + 1) * DN_CHUNK) for c, _ in items}
        g_j = [jnp.broadcast_to(gb_scr[c * SMALL_ROWS + B_HEADS + h:c * SMALL_ROWS + B_HEADS + h + 1, :], sq)
               for c, h in items]
        b_i = [jnp.broadcast_to(gb_scr[c * SMALL_ROWS + h:c * SMALL_ROWS + h + 1, :], sq).T
               for c, h in items]
        g_i = [x.T for x in g_j]
        decay = [jnp.exp(jnp.where(ii >= jj, x - y, -jnp.inf)) for x, y in zip(g_i, g_j)]
        kb = [kn[h, rows[c], :] * b for (c, h), b in zip(items, b_i)]
        a = [jnp.where(ii > jj, _mm_nt(x, kn[h, rows[c], :]) * d, 0.0) for x, (c, h), d in zip(kb, items, decay)]
        t_inv = _unit_lower_inverse(a, ii, jj)
        e_i = [jnp.exp(x) for x in g_i]
        sol = [_mm(t, jnp.concatenate([vn[h, rows[c], :] * b, x * e], axis=1))
               for t, (c, h), b, x, e in zip(t_inv, items, b_i, kb, e_i)]
        attn = [_mm_nt(qn[h, rows[c], :], kn[h, rows[c], :]) * d for (c, h), d in zip(items, decay)]
        for i, (c, h) in enumerate(items):
            r = rows[c]
            g_last = jnp.broadcast_to(g_j[i][:, DN_CHUNK - 1:DN_CHUNK], sq)
            u_scr[h, r, :] = sol[i][:, :HEAD_DIM]
            w_scr[h, r, :] = sol[i][:, HEAD_DIM:].astype(BF16)
            at_scr[h, r, :] = attn[i].astype(BF16)
            qg_scr[h, r, :] = (qn[h, r, :] * e_i[i]).astype(BF16)
            kdt_scr[h, r, :] = (kn[h, r, :] * jnp.exp(g_last - g_i[i])).T.astype(BF16)
            el_scr[h, c:c + 1, :] = jnp.exp(g_last[0:1, :])
        if c0 + DN_GROUP < nchunks:
            conv_chunks(range(c0 + DN_GROUP, c0 + 2 * DN_GROUP))
    for idx in range(3 * B_HEADS):
        xpad[idx, 0:pad, :] = xpad[idx, rows_per_step:rows_per_step + pad, :]

    heads = range(B_HEADS)
    state = [st_scr[h] for h in heads]
    for c in range(nchunks):
        r = slice(c * DN_CHUNK, (c + 1) * DN_CHUNK)
        sb = [s.astype(BF16) for s in state]
        v_new = [u_scr[h, r, :] - jnp.dot(w_scr[h, r, :], sb[h], preferred_element_type=F32) for h in heads]
        vb = [x.astype(BF16) for x in v_new]
        o = [jnp.dot(qg_scr[h, r, :], sb[h], preferred_element_type=F32)
             + jnp.dot(at_scr[h, r, :], vb[h], preferred_element_type=F32) for h in heads]
        state = [state[h] * el_scr[h, c:c + 1, :] + jnp.dot(kdt_scr[h, r, :], vb[h], preferred_element_type=F32)
                 for h in heads]
        for h in heads:
            y = _rms(o[h], og_ref[...]) * _silu(gate_ref[h, r, :].astype(F32))
            o_ref[h, r, :] = y.astype(BF16)
    for h in heads:
        st_scr[h] = state[h]


def _mixer_b(proj, small3, conv_w3, a_log, dt_bias, o_norm_g, *, batch, seq):
    t = proj.shape[1]
    steps = seq // DN_STEP
    nch = DN_STEP // DN_CHUNK
    slab = lambda n: pl.BlockSpec((B_HEADS, DN_STEP, LANES), lambda b, s: (n, b * steps + s, 0))
    smem = pl.BlockSpec(memory_space=pltpu.SMEM)
    per_head = lambda rows, dtype: pltpu.VMEM((B_HEADS, rows, HEAD_DIM), dtype)
    return pl.pallas_call(
        functools.partial(_mixer_b_kernel, rows_per_step=DN_STEP),
        grid=(batch, steps),
        in_specs=[
            smem, smem,
            slab(0), slab(1), slab(2), slab(3),
            pl.BlockSpec((nch, SMALL_ROWS, DN_CHUNK), lambda b, s: (b * steps + s, 0, 0)),
            pl.BlockSpec((3 * B_HEADS, B_CONV, HEAD_DIM), lambda b, s: (0, 0, 0)),
            pl.BlockSpec((1, HEAD_DIM), lambda b, s: (0, 0)),
        ],
        out_specs=pl.BlockSpec((B_HEADS, DN_STEP, LANES), lambda b, s: (0, b * steps + s, 0)),
        out_shape=jax.ShapeDtypeStruct((B_HEADS, t, LANES), BF16),
        scratch_shapes=[
            pltpu.VMEM((3 * B_HEADS, DN_STEP + 8, HEAD_DIM), F32),
            per_head(DN_STEP, F32),
            per_head(DN_STEP, F32),
            per_head(DN_STEP, F32),
            pltpu.VMEM((nch * SMALL_ROWS, DN_CHUNK), F32),
            pltpu.VMEM((B_HEADS, nch, LANES), F32),
            per_head(HEAD_DIM, F32),
            per_head(DN_STEP, F32),
            per_head(DN_STEP, BF16),
            per_head(DN_STEP, BF16),
            per_head(DN_STEP, BF16),
            per_head(DN_STEP, BF16),
        ],
        compiler_params=_params("parallel", "arbitrary"),
        name="mixer_b",
    )(a_log, dt_bias, proj, proj, proj, proj, small3, conv_w3, o_norm_g)


def _mixer_c_kernel(slope_ref, q_ref, k_ref, v_ref, qg_ref, kg_ref, o_ref,
                    qs, ks, vs, ob, lb, *, seq):
    h = pl.program_id(1)
    slope = slope_ref[h]
    qs[...] = _rms(q_ref[...].astype(F32), qg_ref[...]) * (HEAD_DIM ** -0.5)
    ks[...] = _rms(k_ref[...].astype(F32), kg_ref[...])
    vs[...] = v_ref[...].astype(F32)

    qi = lax.broadcasted_iota(jnp.int32, (C_BLOCK, 2 * C_BLOCK), 0)
    kj = lax.broadcasted_iota(jnp.int32, (C_BLOCK, 2 * C_BLOCK), 1)
    delta = C_BLOCK + qi - kj

    for br, (window, dil) in enumerate(C_BRANCHES):
        span = window // dil
        seg = seq // dil
        in_band = (delta >= 0) & (delta <= span)
        bias = jnp.where(in_band, -slope * (delta * dil).astype(F32), -jnp.inf)
        bias_cur = bias[:, C_BLOCK:]

        def rows(start, size, dil=dil):
            return pl.ds(start, size, stride=dil) if dil > 1 else pl.ds(start, size)

        def attend(items, br=br):
            s = [_mm_nt(qs[q, :], ks[k, :]) + b for q, k, b in items]
            mx = [jnp.max(x, axis=-1, keepdims=True) for x in s]
            p = [jnp.exp(x - m).astype(BF16) for x, m in zip(s, mx)]
            acc = [jnp.dot(x, jnp.concatenate([vs[k, :].astype(BF16), jnp.ones((x.shape[1], LANES), BF16)], axis=1),
                           preferred_element_type=F32) for x, (_, k, _) in zip(p, items)]
            for (q, _, _), y, m in zip(items, acc, mx):
                den = y[:, HEAD_DIM:]
                ob[br, q, :] = y[:, :HEAD_DIM] / den
                lb[br, q, :] = m + jnp.log(den)

        nb = seg // C_BLOCK
        blocks = []
        for r in range(dil):
            blocks.append((rows(r, C_BLOCK), rows(r, C_BLOCK), bias_cur))
            for n in range(1, nb):
                blocks.append((rows(r + n * C_BLOCK * dil, C_BLOCK),
                               rows(r + (n - 1) * C_BLOCK * dil, 2 * C_BLOCK), bias))
        for g0 in range(0, len(blocks), C_GROUP):
            attend(blocks[g0:g0 + C_GROUP])

    def merge(n, carry):
        r = pl.ds(pl.multiple_of(n * C_BLOCK, C_BLOCK), C_BLOCK)
        l0, l1, l2 = lb[0, r, :], lb[1, r, :], lb[2, r, :]
        mx = jnp.maximum(jnp.maximum(l0, l1), l2)
        w0, w1, w2 = jnp.exp(l0 - mx), jnp.exp(l1 - mx), jnp.exp(l2 - mx)
        out = (w0 * ob[0, r, :] + w1 * ob[1, r, :] + w2 * ob[2, r, :]) / (w0 + w1 + w2)
        o_ref[r, :] = out.astype(BF16)
        return carry

    lax.fori_loop(0, seq // C_BLOCK, merge, 0)


def _mixer_c(proj, slopes, q_g, k_g, *, batch, seq):
    t = proj.shape[1]
    slab = lambda n: pl.BlockSpec((None, seq, LANES), lambda b, h: (n * C_HEADS + h, b, 0))
    vec = pl.BlockSpec((1, HEAD_DIM), lambda b, h: (0, 0))
    return pl.pallas_call(
        functools.partial(_mixer_c_kernel, seq=seq),
        grid=(batch, C_HEADS),
        in_specs=[pl.BlockSpec(memory_space=pltpu.SMEM),
                  slab(0), slab(1), slab(2), vec, vec],
        out_specs=pl.BlockSpec((None, seq, LANES), lambda b, h: (h, b, 0)),
        out_shape=jax.ShapeDtypeStruct((C_HEADS, t, LANES), BF16),
        scratch_shapes=[
            pltpu.VMEM((seq, HEAD_DIM), F32),
            pltpu.VMEM((seq, HEAD_DIM), F32),
            pltpu.VMEM((seq, HEAD_DIM), F32),
            pltpu.VMEM((len(C_BRANCHES), seq, HEAD_DIM), F32),
            pltpu.VMEM((len(C_BRANCHES), seq, LANES), F32),
        ],
        compiler_params=_params("parallel", "arbitrary"),
        name="mixer_c",
    )(slopes, proj, proj, proj, q_g, k_g)


def _out_proj_kernel(x_ref, ya_ref, yb_ref, yc_ref, w_ref, g_ref, xo_ref, ho_ref):
    mix = jnp.concatenate([ya_ref[i] for i in range(A_GROUPS)]
                          + [yb_ref[i] for i in range(B_HEADS)]
                          + [yc_ref[i] for i in range(C_HEADS)], axis=1)
    xn = x_ref[...] + jnp.dot(mix, w_ref[...], preferred_element_type=F32)
    xo_ref[...] = xn
    ho_ref[...] = _rms(xn, g_ref[...]).astype(BF16)


def _out_proj(x2, ya, yb, yc, w, gain, layer, *, tm=512):
    t = x2.shape[0]
    return pl.pallas_call(
        _out_proj_kernel,
        grid=(t // tm,),
        in_specs=[
            pl.BlockSpec((tm, D_MODEL), lambda i: (i, 0)),
            pl.BlockSpec((A_GROUPS, tm, LANES), lambda i: (0, i, 0)),
            pl.BlockSpec((B_HEADS, tm, LANES), lambda i: (0, i, 0)),
            pl.BlockSpec((C_HEADS, tm, LANES), lambda i: (0, i, 0)),
            pl.BlockSpec((None, D_MODEL, D_MODEL), lambda i: (layer, 0, 0), pipeline_mode=pl.Buffered(1)),
            pl.BlockSpec((None, 1, D_MODEL), lambda i: (layer, 0, 0)),
        ],
        out_specs=[
            pl.BlockSpec((tm, D_MODEL), lambda i: (i, 0)),
            pl.BlockSpec((tm, D_MODEL), lambda i: (i, 0)),
        ],
        out_shape=[
            jax.ShapeDtypeStruct((t, D_MODEL), F32),
            jax.ShapeDtypeStruct((t, D_MODEL), BF16),
        ],
        compiler_params=_params("parallel"),
        name="out_proj",
    )(x2, ya, yb, yc, w, gain)


def _ffn_kernel(h_ref, x_ref, wg_ref, wu_ref, wd_ref, o_ref):
    h = h_ref[...]
    gt = jnp.dot(h, wg_ref[...], preferred_element_type=F32)
    up = jnp.dot(h, wu_ref[...], preferred_element_type=F32)
    act = (_silu(gt) * up).astype(BF16)
    first = pl.program_id(1) == 0

    @pl.when(first)
    def _():
        o_ref[...] = x_ref[...] + jnp.dot(act, wd_ref[...], preferred_element_type=F32)

    @pl.when(jnp.logical_not(first))
    def _():
        o_ref[...] += jnp.dot(act, wd_ref[...], preferred_element_type=F32)


def _ffn(h2, x2, w_gate_up, w_down, layer, *, tm=512, tf=FFN_TILE):
    t = x2.shape[0]
    nf = FFN_HIDDEN // tf
    return pl.pallas_call(
        _ffn_kernel,
        grid=(t // tm, nf),
        in_specs=[
            pl.BlockSpec((tm, D_MODEL), lambda i, j: (i, 0)),
            pl.BlockSpec((tm, D_MODEL), lambda i, j: (i, 0)),
            pl.BlockSpec((None, D_MODEL, tf), lambda i, j: (layer, 0, j)),
            pl.BlockSpec((None, D_MODEL, tf), lambda i, j: (layer, 0, nf + j)),
            pl.BlockSpec((None, tf, D_MODEL), lambda i, j: (layer, j, 0)),
        ],
        out_specs=pl.BlockSpec((tm, D_MODEL), lambda i, j: (i, 0)),
        out_shape=jax.ShapeDtypeStruct((t, D_MODEL), F32),
        compiler_params=_params("parallel", "arbitrary"),
        name="ffn",
    )(h2, x2, w_gate_up, w_gate_up, w_down)


def kernel(x, norm1_g, w_in, sgu_norm_g, w_spatial, b_spatial, conv_w, a_log, dt_bias, o_norm_g,
           q_norm_g, k_norm_g, w_out, norm2_g, w_gate_up, w_down):
    batch, seq, _ = x.shape
    depth = w_in.shape[0]
    t = batch * seq
    slopes = jnp.exp2(-8.0 * (jnp.arange(C_HEADS, dtype=F32) + 1.0) / C_HEADS)
    w_in_t = _cast_w_in_t(w_in)
    w_c = w_in_t[:, OFF_C:, :]
    w_small = jnp.pad(w_in_t[:, OFF_BETA:OFF_C, :], ((0, 0), (0, SMALL_ROWS - 2 * B_HEADS), (0, 0)))
    w_out_b = w_out.astype(BF16)
    w_gate_up_b = w_gate_up.astype(BF16)
    w_down_b = w_down.astype(BF16)
    conv_w3 = conv_w.reshape(depth, B_CONV, 3 * B_HEADS, HEAD_DIM).transpose(0, 2, 1, 3)
    norm1 = norm1_g.reshape(depth, 1, D_MODEL)
    norm2 = norm2_g.reshape(depth, 1, D_MODEL)

    x2 = x.reshape(t, D_MODEL)
    for l in range(depth):
        pa, pb, pc, small3 = _in_proj(x2, norm1, w_in_t, w_c, w_small, l)
        ya = _mixer_a(pa, sgu_norm_g[l], w_spatial[l], b_spatial[l])
        yb = _mixer_b(pb, small3, conv_w3[l], a_log[l], dt_bias[l], o_norm_g[l].reshape(1, HEAD_DIM),
                      batch=batch, seq=seq)
        yc = _mixer_c(pc, slopes, q_norm_g[l].reshape(1, HEAD_DIM), k_norm_g[l].reshape(1, HEAD_DIM),
                      batch=batch, seq=seq)
        x2, h2 = _out_proj(x2, ya, yb, yc, w_out_b, norm2, l)
        x2 = _ffn(h2, x2, w_gate_up_b, w_down_b, l)
    return x2.reshape(batch, seq, D_MODEL)
```

```python
import functools

import jax
import jax.numpy as jnp
from jax import lax
from jax.experimental import pallas as pl
from jax.experimental.pallas import tpu as pltpu

F32 = jnp.float32
BF16 = jnp.bfloat16

D_MODEL = 2048
HEAD_DIM = 128
A_GROUPS = 4
A_WIDTH = A_GROUPS * HEAD_DIM
A_CHUNK = 128
B_HEADS = 6
B_WIDTH = B_HEADS * HEAD_DIM
B_CONV = 4
C_HEADS = 6
C_WIDTH = C_HEADS * HEAD_DIM
C_BRANCHES = ((128, 1), (512, 4), (2048, 16))
C_BLOCK = 128
FFN_HIDDEN = 5632
FFN_TILE = 512
EPS = 1e-6

OFF_B = 2 * A_WIDTH
OFF_BETA = OFF_B + 4 * B_WIDTH
OFF_C = OFF_BETA + 2 * B_HEADS
IN_TOTAL = OFF_C + 3 * C_WIDTH
SLABS_A, SLABS_B, SLABS_C = 2 * A_GROUPS, 4 * B_HEADS, 3 * C_HEADS
SMALL_ROWS = 16
IN_STEPS_AB = (SLABS_A + SLABS_B) // SLABS_A
IN_STEPS_C = 3
IN_NORM_PIECES = 4

LANES = 128
DN_CHUNK = 128
DN_STEP = 512
DN_GROUP = 2
C_GROUP = 8
VMEM_LIMIT = 56 * 1024 * 1024

NT_DIMS = (((1,), (1,)), ((), ()))


def _mm(a, b):
    return jnp.dot(a.astype(BF16), b.astype(BF16), preferred_element_type=F32)


def _mm_nt(a, b):
    return lax.dot_general(a.astype(BF16), b.astype(BF16), NT_DIMS, preferred_element_type=F32)


def _sigmoid(x):
    return 0.5 * (1.0 + jnp.tanh(0.5 * x))


def _silu(x):
    return x * _sigmoid(x)


def _gelu_tanh(x):
    return 0.5 * x * (1.0 + jnp.tanh(0.7978845608028654 * (x + 0.044715 * (x * x * x))))


def _rms(x, gain):
    return x * lax.rsqrt(jnp.mean(x * x, axis=-1, keepdims=True) + EPS) * gain


def _params(*sem):
    return pltpu.CompilerParams(dimension_semantics=sem, vmem_limit_bytes=VMEM_LIMIT)


def _cast_w_in_kernel(w_ref, o_ref, *, tn, total_rows):
    depth, _, d = o_ref.shape
    per_col = depth * (d // LANES)
    grp = 8 * per_col
    i = lax.broadcasted_iota(jnp.int32, (grp, grp), 0)
    c = lax.broadcasted_iota(jnp.int32, (grp, grp), 1)
    perm = jnp.where(c == (i % 8) * per_col + i // 8, 1.0, 0.0).astype(BF16)
    valid = total_rows - pl.program_id(0) * (tn * per_col)
    row = lax.broadcasted_iota(jnp.int32, (grp, LANES), 0)
    regrouped = [jnp.dot(perm, jnp.where(row + g * grp < valid, w_ref[g * grp:(g + 1) * grp, :], 0.0).astype(BF16),
                         preferred_element_type=F32)
                 for g in range(tn // 8)]
    for kt in range(d // LANES):
        for l in range(depth):
            r = kt * depth + l
            rows = jnp.concatenate([y[r * 8:(r + 1) * 8, :] for y in regrouped], axis=0)
            o_ref[l, :, kt * LANES:(kt + 1) * LANES] = rows.astype(BF16)


def _cast_w_in_t(w_in, *, tn=128):
    depth, d, cols = w_in.shape
    rows_view = (jnp.transpose(w_in, (2, 0, 1)).reshape(cols, depth, d // LANES, LANES)
                 .transpose(0, 2, 1, 3).reshape(cols * (d // LANES) * depth, LANES))
    per_col = depth * (d // LANES)
    return pl.pallas_call(
        functools.partial(_cast_w_in_kernel, tn=tn, total_rows=cols * per_col),
        grid=(pl.cdiv(cols, tn),),
        in_specs=[pl.BlockSpec((tn * per_col, LANES), lambda j: (j, 0))],
        out_specs=pl.BlockSpec((depth, tn, d), lambda j: (0, j, 0)),
        out_shape=jax.ShapeDtypeStruct((depth, cols, d), BF16),
        compiler_params=_params("parallel"),
        name="cast_w_in",
    )(rows_view)


def _in_proj_kernel(x_ref, g_ref, wab_ref, wc_ref, ws_ref, oa_ref, ob_ref, oc_ref, os_ref, h_scr):
    j = pl.program_id(1)

    @pl.when(j == 0)
    def _():
        piece = x_ref.shape[0] // IN_NORM_PIECES
        for r in range(IN_NORM_PIECES):
            rows = slice(r * piece, (r + 1) * piece)
            h = _rms(x_ref[rows, :], g_ref[...]).astype(BF16)
            h_scr[rows, :] = h
            res = lax.dot_general(h, wab_ref[...], NT_DIMS, preferred_element_type=F32)
            for c in range(oa_ref.shape[0]):
                oa_ref[c, rows, :] = res[:, c * LANES:(c + 1) * LANES].astype(BF16)
            small = lax.dot_general(ws_ref[...], h, NT_DIMS, preferred_element_type=F32)
            for c in range(piece // DN_CHUNK):
                os_ref[r * (piece // DN_CHUNK) + c] = small[:, c * DN_CHUNK:(c + 1) * DN_CHUNK]

    def project(w_ref, o_ref):
        res = lax.dot_general(h_scr[...], w_ref[...], NT_DIMS, preferred_element_type=F32)
        for c in range(o_ref.shape[0]):
            o_ref[c] = res[:, c * LANES:(c + 1) * LANES].astype(BF16)

    pl.when((j >= 1) & (j <= IN_STEPS_C))(lambda: project(wc_ref, oc_ref))
    pl.when(j > IN_STEPS_C)(lambda: project(wab_ref, ob_ref))


def _in_proj(x2, gain, w_ab, w_c, w_small, layer, *, tm=1024):
    t = x2.shape[0]
    tn_ab = SLABS_A * LANES
    tn_c = (SLABS_C // IN_STEPS_C) * LANES
    clamp = lambda v, hi: jnp.minimum(jnp.maximum(v, 0), hi)
    return pl.pallas_call(
        _in_proj_kernel,
        grid=(t // tm, IN_STEPS_AB + IN_STEPS_C),
        in_specs=[
            pl.BlockSpec((tm, D_MODEL), lambda i, j: (i, 0)),
            pl.BlockSpec((None, 1, D_MODEL), lambda i, j: (layer, 0, 0)),
            pl.BlockSpec((None, tn_ab, D_MODEL), lambda i, j: (layer, clamp(j - IN_STEPS_C, IN_STEPS_AB - 1), 0)),
            pl.BlockSpec((None, tn_c, D_MODEL), lambda i, j: (layer, clamp(j - 1, IN_STEPS_C - 1), 0)),
            pl.BlockSpec((None, SMALL_ROWS, D_MODEL), lambda i, j: (layer, 0, 0)),
        ],
        out_specs=[
            pl.BlockSpec((SLABS_A, tm, LANES), lambda i, j: (0, i, 0)),
            pl.BlockSpec((SLABS_A, tm, LANES), lambda i, j: (clamp(j - IN_STEPS_C - 1, IN_STEPS_AB - 2), i, 0)),
            pl.BlockSpec((SLABS_C // IN_STEPS_C, tm, LANES), lambda i, j: (clamp(j - 1, IN_STEPS_C - 1), i, 0)),
            pl.BlockSpec((tm // DN_CHUNK, SMALL_ROWS, DN_CHUNK), lambda i, j: (i, 0, 0)),
        ],
        out_shape=[
            jax.ShapeDtypeStruct((SLABS_A, t, LANES), BF16),
            jax.ShapeDtypeStruct((SLABS_B, t, LANES), BF16),
            jax.ShapeDtypeStruct((SLABS_C, t, LANES), BF16),
            jax.ShapeDtypeStruct((t // DN_CHUNK, SMALL_ROWS, DN_CHUNK), F32),
        ],
        scratch_shapes=[pltpu.VMEM((tm, D_MODEL), BF16)],
        compiler_params=_params("parallel", "arbitrary"),
        name="in_proj",
    )(x2, gain, w_ab, w_c, w_small)


def _mixer_a_kernel(p_ref, sg_ref, ws_ref, bs_ref, o_ref, *, chunks):
    ii = lax.broadcasted_iota(jnp.int32, (A_CHUNK, A_CHUNK), 0)
    jj = lax.broadcasted_iota(jnp.int32, (A_CHUNK, A_CHUNK), 1)
    for g in range(A_GROUPS):
        w = jnp.where(ii >= jj, ws_ref[g], 0.0).astype(BF16)
        gain = sg_ref[g:g + 1, :]
        bias = bs_ref[g]
        for c in range(chunks):
            rows = slice(c * A_CHUNK, (c + 1) * A_CHUNK)
            u = _gelu_tanh(p_ref[g, rows, :].astype(F32))
            v = _rms(_gelu_tanh(p_ref[A_GROUPS + g, rows, :].astype(F32)), gain)
            z = jnp.dot(w, v.astype(BF16), preferred_element_type=F32) + bias
            o_ref[g, rows, :] = (u * z).astype(BF16)


def _mixer_a(proj, sgu_g, w_s, b_s, *, tm=512):
    t = proj.shape[1]
    return pl.pallas_call(
        functools.partial(_mixer_a_kernel, chunks=tm // A_CHUNK),
        grid=(t // tm,),
        in_specs=[
            pl.BlockSpec((SLABS_A, tm, LANES), lambda i: (0, i, 0)),
            pl.BlockSpec((A_GROUPS, HEAD_DIM), lambda i: (0, 0)),
            pl.BlockSpec((A_GROUPS, A_CHUNK, A_CHUNK), lambda i: (0, 0, 0)),
            pl.BlockSpec((A_GROUPS, A_CHUNK, 1), lambda i: (0, 0, 0)),
        ],
        out_specs=pl.BlockSpec((A_GROUPS, tm, LANES), lambda i: (0, i, 0)),
        out_shape=jax.ShapeDtypeStruct((A_GROUPS, t, LANES), BF16),
        compiler_params=_params("parallel"),
        name="mixer_a",
    )(proj, sgu_g, w_s, b_s.reshape(A_GROUPS, A_CHUNK, 1))


def _unit_lower_inverse(a_list, ii, jj):
    def same_block(shift):
        return (ii >> shift) == (jj >> shift)

    a_pow = [jnp.where(same_block(4), a, 0.0) for a in a_list]
    eye = jnp.where(ii == jj, 1.0, 0.0)
    p = [eye - x for x in a_pow]
    for _ in range(3):
        a_pow = [_mm(x, x) for x in a_pow]
        p = [y + _mm(y, x) for y, x in zip(p, a_pow)]
    for shift in (4, 5, 6):
        off = same_block(shift + 1) & jnp.logical_not(same_block(shift))
        t = [_mm(jnp.where(off, a, 0.0), y) for a, y in zip(a_list, p)]
        p = [y - _mm(y, x) for y, x in zip(p, t)]
    return p


def _mixer_b_kernel(alog_ref, dtb_ref, q_ref, k_ref, v_ref, gate_ref, small_ref, cw_ref, og_ref, o_ref,
                    xpad, qn, kn, vn, gb_scr, el_scr, st_scr, u_scr, w_scr, at_scr, qg_scr, kdt_scr, *, rows_per_step):
    nchunks = rows_per_step // DN_CHUNK
    pad = 8
    sq = (DN_CHUNK, DN_CHUNK)

    @pl.when(pl.program_id(1) == 0)
    def _():
        xpad[:, 0:pad, :] = jnp.zeros((3 * B_HEADS, pad, HEAD_DIM), F32)
        st_scr[...] = jnp.zeros_like(st_scr)

    tensors = ((q_ref, qn), (k_ref, kn), (v_ref, vn))
    for kind, (src, _) in enumerate(tensors):
        for h in range(B_HEADS):
            xpad[kind * B_HEADS + h, pad:pad + rows_per_step, :] = src[h].astype(F32)

    def conv_chunks(chunks):
        for kind, (_, dst) in enumerate(tensors):
            for h in range(B_HEADS):
                idx = kind * B_HEADS + h
                cw = cw_ref[idx]
                for t in chunks:
                    base = pad + t * DN_CHUNK
                    y = cw[B_CONV - 1:B_CONV, :] * xpad[idx, base:base + DN_CHUNK, :]
                    for s in range(1, B_CONV):
                        y = y + cw[B_CONV - 1 - s:B_CONV - s, :] * xpad[idx, base - s:base - s + DN_CHUNK, :]
                    y = _silu(y)
                    if kind < 2:
                        y = y * lax.rsqrt(jnp.sum(y * y, axis=-1, keepdims=True) + EPS)
                    if kind == 0:
                        y = y * (HEAD_DIM ** -0.5)
                    dst[h, t * DN_CHUNK:(t + 1) * DN_CHUNK, :] = y

    small = small_ref[...].reshape(nchunks * SMALL_ROWS, DN_CHUNK)
    j = lax.broadcasted_iota(jnp.int32, small.shape, 0) % SMALL_ROWS
    lane = lax.broadcasted_iota(jnp.int32, small.shape, 1)
    dtb = jnp.zeros_like(small)
    alog = jnp.zeros_like(small)
    for h in range(B_HEADS):
        dtb = jnp.where(j == B_HEADS + h, dtb_ref[h], dtb)
        alog = jnp.where(j == B_HEADS + h, alog_ref[h], alog)
    xa = small + dtb
    g = -jnp.exp(alog) * (jnp.maximum(xa, 0.0) + jnp.log(1.0 + jnp.exp(-jnp.abs(xa))))
    shift = 1
    while shift < DN_CHUNK:
        g = g + jnp.where(lane >= shift, pltpu.roll(g, shift, 1), 0.0)
        shift *= 2
    gb_scr[...] = jnp.where(j < B_HEADS, _sigmoid(small), g)

    ii = lax.broadcasted_iota(jnp.int32, sq, 0)
    jj = lax.broadcasted_iota(jnp.int32, sq, 1)

    conv_chunks(range(DN_GROUP))
    for c0 in range(0, nchunks, DN_GROUP):
        items = [(c, h) for c in range(c0, c0 + DN_GROUP) for h in range(B_HEADS)]
        rows = {c: slice(c * DN_CHUNK, (c + 1) * DN_CHUNK) for c, _ in items}
        g_j = [jnp.broadcast_to(gb_scr[c * SMALL_ROWS + B_HEADS + h:c * SMALL_ROWS + B_HEADS + h + 1, :], sq)
               for c, h in items]
        b_i = [jnp.broadcast_to(gb_scr[c * SMALL_ROWS + h:c * SMALL_ROWS + h + 1, :], sq).T
               for c, h in items]
        g_i = [x.T for x in g_j]
        decay = [jnp.exp(jnp.where(ii >= jj, x - y, -jnp.inf)) for x, y in zip(g_i, g_j)]
        kb = [kn[h, rows[c], :] * b for (c, h), b in zip(items, b_i)]
        a = [jnp.where(ii > jj, _mm_nt(x, kn[h, rows[c], :]) * d, 0.0) for x, (c, h), d in zip(kb, items, decay)]
        t_inv = _unit_lower_inverse(a, ii, jj)
        e_i = [jnp.exp(x) for x in g_i]
        sol = [_mm(t, jnp.concatenate([vn[h, rows[c], :] * b, x * e], axis=1))
               for t, (c, h), b, x, e in zip(t_inv, items, b_i, kb, e_i)]
        attn = [_mm_nt(qn[h, rows[c], :], kn[h, rows[c], :]) * d for (c, h), d in zip(items, decay)]
        for i, (c, h) in enumerate(items):
            r = rows[c]
            g_last = jnp.broadcast_to(g_j[i][:, DN_CHUNK - 1:DN_CHUNK], sq)
            u_scr[h, r, :] = sol[i][:, :HEAD_DIM]
            w_scr[h, r, :] = sol[i][:, HEAD_DIM:].astype(BF16)
            at_scr[h, r, :] = attn[i].astype(BF16)
            qg_scr[h, r, :] = (qn[h, r, :] * e_i[i]).astype(BF16)
            kdt_scr[h, r, :] = (kn[h, r, :] * jnp.exp(g_last - g_i[i])).T.astype(BF16)
            el_scr[h, c:c + 1, :] = jnp.exp(g_last[0:1, :])
        if c0 + DN_GROUP < nchunks:
            conv_chunks(range(c0 + DN_GROUP, c0 + 2 * DN_GROUP))
    for idx in range(3 * B_HEADS):
        xpad[idx, 0:pad, :] = xpad[idx, rows_per_step:rows_per_step + pad, :]

    heads = range(B_HEADS)
    state = [st_scr[h] for h in heads]
    for c in range(nchunks):
        r = slice(c * DN_CHUNK, (c + 1) * DN_CHUNK)
        sb = [s.astype(BF16) for s in state]
        v_new = [u_scr[h, r, :] - jnp.dot(w_scr[h, r, :], sb[h], preferred_element_type=F32) for h in heads]
        vb = [x.astype(BF16) for x in v_new]
        o = [jnp.dot(qg_scr[h, r, :], sb[h], preferred_element_type=F32)
             + jnp.dot(at_scr[h, r, :], vb[h], preferred_element_type=F32) for h in heads]
        state = [state[h] * el_scr[h, c:c + 1, :] + jnp.dot(kdt_scr[h, r, :], vb[h], preferred_element_type=F32)
                 for h in heads]
        for h in heads:
            y = _rms(o[h], og_ref[...]) * _silu(gate_ref[h, r, :].astype(F32))
            o_ref[h, r, :] = y.astype(BF16)
    for h in heads:
        st_scr[h] = state[h]


def _mixer_b(proj, small3, conv_w3, a_log, dt_bias, o_norm_g, *, batch, seq):
    t = proj.shape[1]
    steps = seq // DN_STEP
    nch = DN_STEP // DN_CHUNK
    slab = lambda n: pl.BlockSpec((B_HEADS, DN_STEP, LANES), lambda b, s: (n, b * steps + s, 0))
    smem = pl.BlockSpec(memory_space=pltpu.SMEM)
    per_head = lambda rows, dtype: pltpu.VMEM((B_HEADS, rows, HEAD_DIM), dtype)
    return pl.pallas_call(
        functools.partial(_mixer_b_kernel, rows_per_step=DN_STEP),
        grid=(batch, steps),
        in_specs=[
            smem, smem,
            slab(0), slab(1), slab(2), slab(3),
            pl.BlockSpec((nch, SMALL_ROWS, DN_CHUNK), lambda b, s: (b * steps + s, 0, 0)),
            pl.BlockSpec((3 * B_HEADS, B_CONV, HEAD_DIM), lambda b, s: (0, 0, 0)),
            pl.BlockSpec((1, HEAD_DIM), lambda b, s: (0, 0)),
        ],
        out_specs=pl.BlockSpec((B_HEADS, DN_STEP, LANES), lambda b, s: (0, b * steps + s, 0)),
        out_shape=jax.ShapeDtypeStruct((B_HEADS, t, LANES), BF16),
        scratch_shapes=[
            pltpu.VMEM((3 * B_HEADS, DN_STEP + 8, HEAD_DIM), F32),
            per_head(DN_STEP, F32),
            per_head(DN_STEP, F32),
            per_head(DN_STEP, F32),
            pltpu.VMEM((nch * SMALL_ROWS, DN_CHUNK), F32),
            pltpu.VMEM((B_HEADS, nch, LANES), F32),
            per_head(HEAD_DIM, F32),
            per_head(DN_STEP, F32),
            per_head(DN_STEP, BF16),
            per_head(DN_STEP, BF16),
            per_head(DN_STEP, BF16),
            per_head(DN_STEP, BF16),
        ],
        compiler_params=_params("parallel", "arbitrary"),
        name="mixer_b",
    )(a_log, dt_bias, proj, proj, proj, proj, small3, conv_w3, o_norm_g)


def _mixer_c_kernel(slope_ref, q_ref, k_ref, v_ref, qg_ref, kg_ref, o_ref,
                    qs, ks, vs, ob, lb, *, seq):
    h = pl.program_id(1)
    slope = slope_ref[h]
    qs[...] = _rms(q_ref[...].astype(F32), qg_ref[...]) * (HEAD_DIM ** -0.5)
    ks[...] = _rms(k_ref[...].astype(F32), kg_ref[...])
    vs[...] = v_ref[...].astype(F32)

    qi = lax.broadcasted_iota(jnp.int32, (C_BLOCK, 2 * C_BLOCK), 0)
    kj = lax.broadcasted_iota(jnp.int32, (C_BLOCK, 2 * C_BLOCK), 1)
    delta = C_BLOCK + qi - kj

    for br, (window, dil) in enumerate(C_BRANCHES):
        span = window // dil
        seg = seq // dil
        in_band = (delta >= 0) & (delta <= span)
        bias = jnp.where(in_band, -slope * (delta * dil).astype(F32), -jnp.inf)
        bias_cur = bias[:, C_BLOCK:]

        def rows(start, size, dil=dil):
            return pl.ds(start, size, stride=dil) if dil > 1 else pl.ds(start, size)

        def attend(items, br=br):
            s = [_mm_nt(qs[q, :], ks[k, :]) + b for q, k, b in items]
            mx = [jnp.max(x, axis=-1, keepdims=True) for x in s]
            p = [jnp.exp(x - m).astype(BF16) for x, m in zip(s, mx)]
            acc = [jnp.dot(x, jnp.concatenate([vs[k, :].astype(BF16), jnp.ones((x.shape[1], LANES), BF16)], axis=1),
                           preferred_element_type=F32) for x, (_, k, _) in zip(p, items)]
            for (q, _, _), y, m in zip(items, acc, mx):
                den = y[:, HEAD_DIM:]
                ob[br, q, :] = y[:, :HEAD_DIM] / den
                lb[br, q, :] = m + jnp.log(den)

        nb = seg // C_BLOCK
        blocks = []
        for r in range(dil):
            blocks.append((rows(r, C_BLOCK), rows(r, C_BLOCK), bias_cur))
            for n in range(1, nb):
                blocks.append((rows(r + n * C_BLOCK * dil, C_BLOCK),
                               rows(r + (n - 1) * C_BLOCK * dil, 2 * C_BLOCK), bias))
        for g0 in range(0, len(blocks), C_GROUP):
            attend(blocks[g0:g0 + C_GROUP])

    def merge(n, carry):
        r = pl.ds(pl.multiple_of(n * C_BLOCK, C_BLOCK), C_BLOCK)
        l0, l1, l2 = lb[0, r, :], lb[1, r, :], lb[2, r, :]
        mx = jnp.maximum(jnp.maximum(l0, l1), l2)
        w0, w1, w2 = jnp.exp(l0 - mx), jnp.exp(l1 - mx), jnp.exp(l2 - mx)
        out = (w0 * ob[0, r, :] + w1 * ob[1, r, :] + w2 * ob[2, r, :]) / (w0 + w1 + w2)
        o_ref[r, :] = out.astype(BF16)
        return carry

    lax.fori_loop(0, seq // C_BLOCK, merge, 0)


def _mixer_c(proj, slopes, q_g, k_g, *, batch, seq):
    t = proj.shape[1]
    slab = lambda n: pl.BlockSpec((None, seq, LANES), lambda b, h: (n * C_HEADS + h, b, 0))
    vec = pl.BlockSpec((1, HEAD_DIM), lambda b, h: (0, 0))
    return pl.pallas_call(
        functools.partial(_mixer_c_kernel, seq=seq),
        grid=(batch, C_HEADS),
        in_specs=[pl.BlockSpec(memory_space=pltpu.SMEM),
                  slab(0), slab(1), slab(2), vec, vec],
        out_specs=pl.BlockSpec((None, seq, LANES), lambda b, h: (h, b, 0)),
        out_shape=jax.ShapeDtypeStruct((C_HEADS, t, LANES), BF16),
        scratch_shapes=[
            pltpu.VMEM((seq, HEAD_DIM), F32),
            pltpu.VMEM((seq, HEAD_DIM), F32),
            pltpu.VMEM((seq, HEAD_DIM), F32),
            pltpu.VMEM((len(C_BRANCHES), seq, HEAD_DIM), F32),
            pltpu.VMEM((len(C_BRANCHES), seq, LANES), F32),
        ],
        compiler_params=_params("parallel", "arbitrary"),
        name="mixer_c",
    )(slopes, proj, proj, proj, q_g, k_g)


def _out_proj_kernel(x_ref, ya_ref, yb_ref, yc_ref, w_ref, g_ref, xo_ref, ho_ref):
    mix = jnp.concatenate([ya_ref[i] for i in range(A_GROUPS)]
                          + [yb_ref[i] for i in range(B_HEADS)]
                          + [yc_ref[i] for i in range(C_HEADS)], axis=1)
    xn = x_ref[...] + jnp.dot(mix, w_ref[...], preferred_element_type=F32)
    xo_ref[...] = xn
    ho_ref[...] = _rms(xn, g_ref[...]).astype(BF16)


def _out_proj(x2, ya, yb, yc, w, gain, layer, *, tm=512):
    t = x2.shape[0]
    return pl.pallas_call(
        _out_proj_kernel,
        grid=(t // tm,),
        in_specs=[
            pl.BlockSpec((tm, D_MODEL), lambda i: (i, 0)),
            pl.BlockSpec((A_GROUPS, tm, LANES), lambda i: (0, i, 0)),
            pl.BlockSpec((B_HEADS, tm, LANES), lambda i: (0, i, 0)),
            pl.BlockSpec((C_HEADS, tm, LANES), lambda i: (0, i, 0)),
            pl.BlockSpec((None, D_MODEL, D_MODEL), lambda i: (layer, 0, 0), pipeline_mode=pl.Buffered(1)),
            pl.BlockSpec((None, 1, D_MODEL), lambda i: (layer, 0, 0)),
        ],
        out_specs=[
            pl.BlockSpec((tm, D_MODEL), lambda i: (i, 0)),
            pl.BlockSpec((tm, D_MODEL), lambda i: (i, 0)),
        ],
        out_shape=[
            jax.ShapeDtypeStruct((t, D_MODEL), F32),
            jax.ShapeDtypeStruct((t, D_MODEL), BF16),
        ],
        compiler_params=_params("parallel"),
        name="out_proj",
    )(x2, ya, yb, yc, w, gain)


def _ffn_kernel(h_ref, x_ref, wg_ref, wu_ref, wd_ref, o_ref):
    @pl.when(pl.program_id(1) == 0)
    def _():
        o_ref[...] = x_ref[...]

    h = h_ref[...]
    gt = jnp.dot(h, wg_ref[...], preferred_element_type=F32)
    up = jnp.dot(h, wu_ref[...], preferred_element_type=F32)
    act = (_silu(gt) * up).astype(BF16)
    o_ref[...] += jnp.dot(act, wd_ref[...], preferred_element_type=F32)


def _ffn(h2, x2, w_gate_up, w_down, layer, *, tm=512, tf=FFN_TILE):
    t = x2.shape[0]
    nf = FFN_HIDDEN // tf
    return pl.pallas_call(
        _ffn_kernel,
        grid=(t // tm, nf),
        in_specs=[
            pl.BlockSpec((tm, D_MODEL), lambda i, j: (i, 0)),
            pl.BlockSpec((tm, D_MODEL), lambda i, j: (i, 0)),
            pl.BlockSpec((None, D_MODEL, tf), lambda i, j: (layer, 0, j)),
            pl.BlockSpec((None, D_MODEL, tf), lambda i, j: (layer, 0, nf + j)),
            pl.BlockSpec((None, tf, D_MODEL), lambda i, j: (layer, j, 0)),
        ],
        out_specs=pl.BlockSpec((tm, D_MODEL), lambda i, j: (i, 0)),
        out_shape=jax.ShapeDtypeStruct((t, D_MODEL), F32),
        compiler_params=_params("parallel", "arbitrary"),
        name="ffn",
    )(h2, x2, w_gate_up, w_gate_up, w_down)


def kernel(x, norm1_g, w_in, sgu_norm_g, w_spatial, b_spatial, conv_w, a_log, dt_bias, o_norm_g,
           q_norm_g, k_norm_g, w_out, norm2_g, w_gate_up, w_down):
    batch, seq, _ = x.shape
    depth = w_in.shape[0]
    t = batch * seq
    slopes = jnp.exp2(-8.0 * (jnp.arange(C_HEADS, dtype=F32) + 1.0) / C_HEADS)
    w_in_t = _cast_w_in_t(w_in)
    w_c = w_in_t[:, OFF_C:, :]
    w_small = jnp.pad(w_in_t[:, OFF_BETA:OFF_C, :], ((0, 0), (0, SMALL_ROWS - 2 * B_HEADS), (0, 0)))
    w_out_b = w_out.astype(BF16)
    w_gate_up_b = w_gate_up.astype(BF16)
    w_down_b = w_down.astype(BF16)
    conv_w3 = conv_w.reshape(depth, B_CONV, 3 * B_HEADS, HEAD_DIM).transpose(0, 2, 1, 3)
    norm1 = norm1_g.reshape(depth, 1, D_MODEL)
    norm2 = norm2_g.reshape(depth, 1, D_MODEL)

    x2 = x.reshape(t, D_MODEL)
    for l in range(depth):
        pa, pb, pc, small3 = _in_proj(x2, norm1, w_in_t, w_c, w_small, l)
        ya = _mixer_a(pa, sgu_norm_g[l], w_spatial[l], b_spatial[l])
        yb = _mixer_b(pb, small3, conv_w3[l], a_log[l], dt_bias[l], o_norm_g[l].reshape(1, HEAD_DIM),
                      batch=batch, seq=seq)
        yc = _mixer_c(pc, slopes, q_norm_g[l].reshape(1, HEAD_DIM), k_norm_g[l].reshape(1, HEAD_DIM),
                      batch=batch, seq=seq)
        x2, h2 = _out_proj(x2, ya, yb, yc, w_out_b, norm2, l)
        x2 = _ffn(h2, x2, w_gate_up_b, w_down_b, l)
    return x2.reshape(batch, seq, D_MODEL)
```

```python
import functools

import jax
import jax.numpy as jnp
from jax import lax
from jax.experimental import pallas as pl
from jax.experimental.pallas import tpu as pltpu

F32 = jnp.float32
BF16 = jnp.bfloat16

D_MODEL = 2048
HEAD_DIM = 128
A_GROUPS = 4
A_WIDTH = A_GROUPS * HEAD_DIM
A_CHUNK = 128
B_HEADS = 6
B_WIDTH = B_HEADS * HEAD_DIM
B_CONV = 4
C_HEADS = 6
C_WIDTH = C_HEADS * HEAD_DIM
C_BRANCHES = ((128, 1), (512, 4), (2048, 16))
C_BLOCK = 128
FFN_HIDDEN = 5632
FFN_TILE = 512
EPS = 1e-6

OFF_B = 2 * A_WIDTH
OFF_BETA = OFF_B + 4 * B_WIDTH
OFF_C = OFF_BETA + 2 * B_HEADS
IN_TOTAL = OFF_C + 3 * C_WIDTH
SLABS_A, SLABS_B, SLABS_C = 2 * A_GROUPS, 4 * B_HEADS, 3 * C_HEADS
SMALL_ROWS = 16
IN_STEPS_AB = (SLABS_A + SLABS_B) // SLABS_A
IN_STEPS_C = 3
IN_NORM_PIECES = 4

LANES = 128
DN_CHUNK = 128
DN_STEP = 512
DN_GROUP = 2
C_GROUP = 8
VMEM_LIMIT = 56 * 1024 * 1024

NT_DIMS = (((1,), (1,)), ((), ()))


def _mm(a, b):
    return jnp.dot(a.astype(BF16), b.astype(BF16), preferred_element_type=F32)


def _mm_nt(a, b):
    return lax.dot_general(a.astype(BF16), b.astype(BF16), NT_DIMS, preferred_element_type=F32)


def _sigmoid(x):
    return 0.5 * (1.0 + jnp.tanh(0.5 * x))


def _silu(x):
    return x * _sigmoid(x)


def _gelu_tanh(x):
    return 0.5 * x * (1.0 + jnp.tanh(0.7978845608028654 * (x + 0.044715 * (x * x * x))))


def _rms(x, gain):
    return x * lax.rsqrt(jnp.mean(x * x, axis=-1, keepdims=True) + EPS) * gain


def _params(*sem):
    return pltpu.CompilerParams(dimension_semantics=sem, vmem_limit_bytes=VMEM_LIMIT)


def _cast_w_in_kernel(w_ref, o_ref, *, tn, total_rows):
    depth, _, d = o_ref.shape
    per_col = depth * (d // LANES)
    grp = 8 * per_col
    i = lax.broadcasted_iota(jnp.int32, (grp, grp), 0)
    c = lax.broadcasted_iota(jnp.int32, (grp, grp), 1)
    perm = jnp.where(c == (i % 8) * per_col + i // 8, 1.0, 0.0).astype(BF16)
    valid = total_rows - pl.program_id(0) * (tn * per_col)
    row = lax.broadcasted_iota(jnp.int32, (grp, LANES), 0)
    regrouped = [jnp.dot(perm, jnp.where(row + g * grp < valid, w_ref[g * grp:(g + 1) * grp, :], 0.0).astype(BF16),
                         preferred_element_type=F32)
                 for g in range(tn // 8)]
    for kt in range(d // LANES):
        for l in range(depth):
            r = kt * depth + l
            rows = jnp.concatenate([y[r * 8:(r + 1) * 8, :] for y in regrouped], axis=0)
            o_ref[l, :, kt * LANES:(kt + 1) * LANES] = rows.astype(BF16)


def _cast_w_in_t(w_in, *, tn=128):
    depth, d, cols = w_in.shape
    rows_view = (jnp.transpose(w_in, (2, 0, 1)).reshape(cols, depth, d // LANES, LANES)
                 .transpose(0, 2, 1, 3).reshape(cols * (d // LANES) * depth, LANES))
    per_col = depth * (d // LANES)
    return pl.pallas_call(
        functools.partial(_cast_w_in_kernel, tn=tn, total_rows=cols * per_col),
        grid=(pl.cdiv(cols, tn),),
        in_specs=[pl.BlockSpec((tn * per_col, LANES), lambda j: (j, 0))],
        out_specs=pl.BlockSpec((depth, tn, d), lambda j: (0, j, 0)),
        out_shape=jax.ShapeDtypeStruct((depth, cols, d), BF16),
        compiler_params=_params("parallel"),
        name="cast_w_in",
    )(rows_view)


def _in_proj_kernel(x_ref, g_ref, wab_ref, wc_ref, ws_ref, oa_ref, ob_ref, oc_ref, os_ref, h_scr):
    j = pl.program_id(1)

    @pl.when(j == 0)
    def _():
        piece = x_ref.shape[0] // IN_NORM_PIECES
        for r in range(IN_NORM_PIECES):
            rows = slice(r * piece, (r + 1) * piece)
            h = _rms(x_ref[rows, :], g_ref[...]).astype(BF16)
            h_scr[rows, :] = h
            res = lax.dot_general(h, wab_ref[...], NT_DIMS, preferred_element_type=F32)
            for c in range(oa_ref.shape[0]):
                oa_ref[c, rows, :] = res[:, c * LANES:(c + 1) * LANES].astype(BF16)
            small = lax.dot_general(ws_ref[...], h, NT_DIMS, preferred_element_type=F32)
            for c in range(piece // DN_CHUNK):
                os_ref[r * (piece // DN_CHUNK) + c] = small[:, c * DN_CHUNK:(c + 1) * DN_CHUNK]

    def project(w_ref, o_ref):
        res = lax.dot_general(h_scr[...], w_ref[...], NT_DIMS, preferred_element_type=F32)
        for c in range(o_ref.shape[0]):
            o_ref[c] = res[:, c * LANES:(c + 1) * LANES].astype(BF16)

    pl.when((j >= 1) & (j <= IN_STEPS_C))(lambda: project(wc_ref, oc_ref))
    pl.when(j > IN_STEPS_C)(lambda: project(wab_ref, ob_ref))


def _in_proj(x2, gain, w_ab, w_c, w_small, layer, *, tm=1024):
    t = x2.shape[0]
    tn_ab = SLABS_A * LANES
    tn_c = (SLABS_C // IN_STEPS_C) * LANES
    clamp = lambda v, hi: jnp.minimum(jnp.maximum(v, 0), hi)
    return pl.pallas_call(
        _in_proj_kernel,
        grid=(t // tm, IN_STEPS_AB + IN_STEPS_C),
        in_specs=[
            pl.BlockSpec((tm, D_MODEL), lambda i, j: (i, 0)),
            pl.BlockSpec((None, 1, D_MODEL), lambda i, j: (layer, 0, 0)),
            pl.BlockSpec((None, tn_ab, D_MODEL), lambda i, j: (layer, clamp(j - IN_STEPS_C, IN_STEPS_AB - 1), 0)),
            pl.BlockSpec((None, tn_c, D_MODEL), lambda i, j: (layer, clamp(j - 1, IN_STEPS_C - 1), 0)),
            pl.BlockSpec((None, SMALL_ROWS, D_MODEL), lambda i, j: (layer, 0, 0)),
        ],
        out_specs=[
            pl.BlockSpec((SLABS_A, tm, LANES), lambda i, j: (0, i, 0)),
            pl.BlockSpec((SLABS_A, tm, LANES), lambda i, j: (clamp(j - IN_STEPS_C - 1, IN_STEPS_AB - 2), i, 0)),
            pl.BlockSpec((SLABS_C // IN_STEPS_C, tm, LANES), lambda i, j: (clamp(j - 1, IN_STEPS_C - 1), i, 0)),
            pl.BlockSpec((tm // DN_CHUNK, SMALL_ROWS, DN_CHUNK), lambda i, j: (i, 0, 0)),
        ],
        out_shape=[
            jax.ShapeDtypeStruct((SLABS_A, t, LANES), BF16),
            jax.ShapeDtypeStruct((SLABS_B, t, LANES), BF16),
            jax.ShapeDtypeStruct((SLABS_C, t, LANES), BF16),
            jax.ShapeDtypeStruct((t // DN_CHUNK, SMALL_ROWS, DN_CHUNK), F32),
        ],
        scratch_shapes=[pltpu.VMEM((tm, D_MODEL), BF16)],
        compiler_params=_params("parallel", "arbitrary"),
        name="in_proj",
    )(x2, gain, w_ab, w_c, w_small)


def _mixer_a_kernel(p_ref, sg_ref, ws_ref, bs_ref, o_ref, *, chunks):
    ii = lax.broadcasted_iota(jnp.int32, (A_CHUNK, A_CHUNK), 0)
    jj = lax.broadcasted_iota(jnp.int32, (A_CHUNK, A_CHUNK), 1)
    for g in range(A_GROUPS):
        w = jnp.where(ii >= jj, ws_ref[g], 0.0).astype(BF16)
        gain = sg_ref[g:g + 1, :]
        bias = bs_ref[g]
        for c in range(chunks):
            rows = slice(c * A_CHUNK, (c + 1) * A_CHUNK)
            u = _gelu_tanh(p_ref[g, rows, :].astype(F32))
            v = _rms(_gelu_tanh(p_ref[A_GROUPS + g, rows, :].astype(F32)), gain)
            z = jnp.dot(w, v.astype(BF16), preferred_element_type=F32) + bias
            o_ref[g, rows, :] = (u * z).astype(BF16)


def _mixer_a(proj, sgu_g, w_s, b_s, *, tm=512):
    t = proj.shape[1]
    return pl.pallas_call(
        functools.partial(_mixer_a_kernel, chunks=tm // A_CHUNK),
        grid=(t // tm,),
        in_specs=[
            pl.BlockSpec((SLABS_A, tm, LANES), lambda i: (0, i, 0)),
            pl.BlockSpec((A_GROUPS, HEAD_DIM), lambda i: (0, 0)),
            pl.BlockSpec((A_GROUPS, A_CHUNK, A_CHUNK), lambda i: (0, 0, 0)),
            pl.BlockSpec((A_GROUPS, A_CHUNK, 1), lambda i: (0, 0, 0)),
        ],
        out_specs=pl.BlockSpec((A_GROUPS, tm, LANES), lambda i: (0, i, 0)),
        out_shape=jax.ShapeDtypeStruct((A_GROUPS, t, LANES), BF16),
        compiler_params=_params("parallel"),
        name="mixer_a",
    )(proj, sgu_g, w_s, b_s.reshape(A_GROUPS, A_CHUNK, 1))


def _unit_lower_inverse(a_list, ii, jj):
    def same_block(shift):
        return (ii >> shift) == (jj >> shift)

    a_pow = [jnp.where(same_block(4), a, 0.0) for a in a_list]
    eye = jnp.where(ii == jj, 1.0, 0.0)
    p = [eye - x for x in a_pow]
    for _ in range(3):
        a_pow = [_mm(x, x) for x in a_pow]
        p = [y + _mm(y, x) for y, x in zip(p, a_pow)]
    for shift in (4, 5, 6):
        off = same_block(shift + 1) & jnp.logical_not(same_block(shift))
        t = [_mm(jnp.where(off, a, 0.0), y) for a, y in zip(a_list, p)]
        p = [y - _mm(y, x) for y, x in zip(p, t)]
    return p


def _mixer_b_kernel(alog_ref, dtb_ref, q_ref, k_ref, v_ref, gate_ref, small_ref, cw_ref, og_ref, o_ref,
                    xpad, qn, kn, vn, gb_scr, el_scr, st_scr, u_scr, w_scr, at_scr, qg_scr, kdt_scr, *, rows_per_step):
    nchunks = rows_per_step // DN_CHUNK
    pad = 8
    sq = (DN_CHUNK, DN_CHUNK)

    @pl.when(pl.program_id(1) == 0)
    def _():
        xpad[:, 0:pad, :] = jnp.zeros((3 * B_HEADS, pad, HEAD_DIM), F32)
        st_scr[...] = jnp.zeros_like(st_scr)

    tensors = ((q_ref, qn), (k_ref, kn), (v_ref, vn))
    for kind, (src, _) in enumerate(tensors):
        for h in range(B_HEADS):
            xpad[kind * B_HEADS + h, pad:pad + rows_per_step, :] = src[h].astype(F32)

    def conv_chunks(chunks):
        for kind, (_, dst) in enumerate(tensors):
            for h in range(B_HEADS):
                idx = kind * B_HEADS + h
                cw = cw_ref[idx]
                for t in chunks:
                    base = pad + t * DN_CHUNK
                    y = cw[B_CONV - 1:B_CONV, :] * xpad[idx, base:base + DN_CHUNK, :]
                    for s in range(1, B_CONV):
                        y = y + cw[B_CONV - 1 - s:B_CONV - s, :] * xpad[idx, base - s:base - s + DN_CHUNK, :]
                    y = _silu(y)
                    if kind < 2:
                        y = y * lax.rsqrt(jnp.sum(y * y, axis=-1, keepdims=True) + EPS)
                    if kind == 0:
                        y = y * (HEAD_DIM ** -0.5)
                    dst[h, t * DN_CHUNK:(t + 1) * DN_CHUNK, :] = y

    small = small_ref[...].reshape(nchunks * SMALL_ROWS, DN_CHUNK)
    j = lax.broadcasted_iota(jnp.int32, small.shape, 0) % SMALL_ROWS
    lane = lax.broadcasted_iota(jnp.int32, small.shape, 1)
    dtb = jnp.zeros_like(small)
    alog = jnp.zeros_like(small)
    for h in range(B_HEADS):
        dtb = jnp.where(j == B_HEADS + h, dtb_ref[h], dtb)
        alog = jnp.where(j == B_HEADS + h, alog_ref[h], alog)
    xa = small + dtb
    g = -jnp.exp(alog) * (jnp.maximum(xa, 0.0) + jnp.log(1.0 + jnp.exp(-jnp.abs(xa))))
    shift = 1
    while shift < DN_CHUNK:
        g = g + jnp.where(lane >= shift, pltpu.roll(g, shift, 1), 0.0)
        shift *= 2
    gb_scr[...] = jnp.where(j < B_HEADS, _sigmoid(small), g)

    ii = lax.broadcasted_iota(jnp.int32, sq, 0)
    jj = lax.broadcasted_iota(jnp.int32, sq, 1)

    conv_chunks(range(DN_GROUP))
    for c0 in range(0, nchunks, DN_GROUP):
        items = [(c, h) for c in range(c0, c0 + DN_GROUP) for h in range(B_HEADS)]
        rows = {c: slice(c * DN_CHUNK, (c + 1) * DN_CHUNK) for c, _ in items}
        g_j = [jnp.broadcast_to(gb_scr[c * SMALL_ROWS + B_HEADS + h:c * SMALL_ROWS + B_HEADS + h + 1, :], sq)
               for c, h in items]
        b_i = [jnp.broadcast_to(gb_scr[c * SMALL_ROWS + h:c * SMALL_ROWS + h + 1, :], sq).T
               for c, h in items]
        g_i = [x.T for x in g_j]
        decay = [jnp.exp(jnp.where(ii >= jj, x - y, -jnp.inf)) for x, y in zip(g_i, g_j)]
        kb = [kn[h, rows[c], :] * b for (c, h), b in zip(items, b_i)]
        a = [jnp.where(ii > jj, _mm_nt(x, kn[h, rows[c], :]) * d, 0.0) for x, (c, h), d in zip(kb, items, decay)]
        t_inv = _unit_lower_inverse(a, ii, jj)
        e_i = [jnp.exp(x) for x in g_i]
        sol = [_mm(t, jnp.concatenate([vn[h, rows[c], :] * b, x * e], axis=1))
               for t, (c, h), b, x, e in zip(t_inv, items, b_i, kb, e_i)]
        attn = [_mm_nt(qn[h, rows[c], :], kn[h, rows[c], :]) * d for (c, h), d in zip(items, decay)]
        for i, (c, h) in enumerate(items):
            r = rows[c]
            g_last = jnp.broadcast_to(g_j[i][:, DN_CHUNK - 1:DN_CHUNK], sq)
            u_scr[h, r, :] = sol[i][:, :HEAD_DIM]
            w_scr[h, r, :] = sol[i][:, HEAD_DIM:].astype(BF16)
            at_scr[h, r, :] = attn[i].astype(BF16)
            qg_scr[h, r, :] = (qn[h, r, :] * e_i[i]).astype(BF16)
            kdt_scr[h, r, :] = (kn[h, r, :] * jnp.exp(g_last - g_i[i])).T.astype(BF16)
            el_scr[h, c:c + 1, :] = jnp.exp(g_last[0:1, :])
        if c0 + DN_GROUP < nchunks:
            conv_chunks(range(c0 + DN_GROUP, c0 + 2 * DN_GROUP))
    for idx in range(3 * B_HEADS):
        xpad[idx, 0:pad, :] = xpad[idx, rows_per_step:rows_per_step + pad, :]

    heads = range(B_HEADS)
    state = [st_scr[h] for h in heads]
    for c in range(nchunks):
        r = slice(c * DN_CHUNK, (c + 1) * DN_CHUNK)
        sb = [s.astype(BF16) for s in state]
        v_new = [u_scr[h, r, :] - jnp.dot(w_scr[h, r, :], sb[h], preferred_element_type=F32) for h in heads]
        vb = [x.astype(BF16) for x in v_new]
        o = [jnp.dot(qg_scr[h, r, :], sb[h], preferred_element_type=F32)
             + jnp.dot(at_scr[h, r, :], vb[h], preferred_element_type=F32) for h in heads]
        state = [state[h] * el_scr[h, c:c + 1, :] + jnp.dot(kdt_scr[h, r, :], vb[h], preferred_element_type=F32)
                 for h in heads]
        for h in heads:
            y = _rms(o[h], og_ref[...]) * _silu(gate_ref[h, r, :].astype(F32))
            o_ref[h, r, :] = y.astype(BF16)
    for h in heads:
        st_scr[h] = state[h]


def _mixer_b(proj, small3, conv_w3, a_log, dt_bias, o_norm_g, *, batch, seq):
    t = proj.shape[1]
    steps = seq // DN_STEP
    nch = DN_STEP // DN_CHUNK
    slab = lambda n: pl.BlockSpec((B_HEADS, DN_STEP, LANES), lambda b, s: (n, b * steps + s, 0))
    smem = pl.BlockSpec(memory_space=pltpu.SMEM)
    per_head = lambda rows, dtype: pltpu.VMEM((B_HEADS, rows, HEAD_DIM), dtype)
    return pl.pallas_call(
        functools.partial(_mixer_b_kernel, rows_per_step=DN_STEP),
        grid=(batch, steps),
        in_specs=[
            smem, smem,
            slab(0), slab(1), slab(2), slab(3),
            pl.BlockSpec((nch, SMALL_ROWS, DN_CHUNK), lambda b, s: (b * steps + s, 0, 0)),
            pl.BlockSpec((3 * B_HEADS, B_CONV, HEAD_DIM), lambda b, s: (0, 0, 0)),
            pl.BlockSpec((1, HEAD_DIM), lambda b, s: (0, 0)),
        ],
        out_specs=pl.BlockSpec((B_HEADS, DN_STEP, LANES), lambda b, s: (0, b * steps + s, 0)),
        out_shape=jax.ShapeDtypeStruct((B_HEADS, t, LANES), BF16),
        scratch_shapes=[
            pltpu.VMEM((3 * B_HEADS, DN_STEP + 8, HEAD_DIM), F32),
            per_head(DN_STEP, F32),
            per_head(DN_STEP, F32),
            per_head(DN_STEP, F32),
            pltpu.VMEM((nch * SMALL_ROWS, DN_CHUNK), F32),
            pltpu.VMEM((B_HEADS, nch, LANES), F32),
            per_head(HEAD_DIM, F32),
            per_head(DN_STEP, F32),
            per_head(DN_STEP, BF16),
            per_head(DN_STEP, BF16),
            per_head(DN_STEP, BF16),
            per_head(DN_STEP, BF16),
        ],
        compiler_params=_params("parallel", "arbitrary"),
        name="mixer_b",
    )(a_log, dt_bias, proj, proj, proj, proj, small3, conv_w3, o_norm_g)


def _mixer_c_kernel(slope_ref, q_ref, k_ref, v_ref, qg_ref, kg_ref, o_ref,
                    qs, ks, vs, ob, lb, *, seq):
    h = pl.program_id(1)
    slope = slope_ref[h]
    qs[...] = _rms(q_ref[...].astype(F32), qg_ref[...]) * (HEAD_DIM ** -0.5)
    ks[...] = _rms(k_ref[...].astype(F32), kg_ref[...])
    vs[...] = v_ref[...].astype(F32)

    qi = lax.broadcasted_iota(jnp.int32, (C_BLOCK, 2 * C_BLOCK), 0)
    kj = lax.broadcasted_iota(jnp.int32, (C_BLOCK, 2 * C_BLOCK), 1)
    delta = C_BLOCK + qi - kj

    for br, (window, dil) in enumerate(C_BRANCHES):
        span = window // dil
        seg = seq // dil
        in_band = (delta >= 0) & (delta <= span)
        bias = jnp.where(in_band, -slope * (delta * dil).astype(F32), -jnp.inf)
        bias_cur = bias[:, C_BLOCK:]

        def rows(start, size, dil=dil):
            return pl.ds(start, size, stride=dil) if dil > 1 else pl.ds(start, size)

        def attend(items, br=br):
            s = [_mm_nt(qs[q, :], ks[k, :]) + b for q, k, b in items]
            mx = [jnp.max(x, axis=-1, keepdims=True) for x in s]
            p = [jnp.exp(x - m).astype(BF16) for x, m in zip(s, mx)]
            acc = [jnp.dot(x, jnp.concatenate([vs[k, :].astype(BF16), jnp.ones((x.shape[1], LANES), BF16)], axis=1),
                           preferred_element_type=F32) for x, (_, k, _) in zip(p, items)]
            for (q, _, _), y, m in zip(items, acc, mx):
                den = y[:, HEAD_DIM:]
                ob[br, q, :] = y[:, :HEAD_DIM] / den
                lb[br, q, :] = m + jnp.log(den)

        nb = seg // C_BLOCK
        blocks = []
        for r in range(dil):
            blocks.append((rows(r, C_BLOCK), rows(r, C_BLOCK), bias_cur))
            for n in range(1, nb):
                blocks.append((rows(r + n * C_BLOCK * dil, C_BLOCK),
                               rows(r + (n - 1) * C_BLOCK * dil, 2 * C_BLOCK), bias))
        for g0 in range(0, len(blocks), C_GROUP):
            attend(blocks[g0:g0 + C_GROUP])

    def merge(n, carry):
        r = pl.ds(pl.multiple_of(n * C_BLOCK, C_BLOCK), C_BLOCK)
        l0, l1, l2 = lb[0, r, :], lb[1, r, :], lb[2, r, :]
        mx = jnp.maximum(jnp.maximum(l0, l1), l2)
        w0, w1, w2 = jnp.exp(l0 - mx), jnp.exp(l1 - mx), jnp.exp(l2 - mx)
        out = (w0 * ob[0, r, :] + w1 * ob[1, r, :] + w2 * ob[2, r, :]) / (w0 + w1 + w2)
        o_ref[r, :] = out.astype(BF16)
        return carry

    lax.fori_loop(0, seq // C_BLOCK, merge, 0)


def _mixer_c(proj, slopes, q_g, k_g, *, batch, seq):
    t = proj.shape[1]
    slab = lambda n: pl.BlockSpec((None, seq, LANES), lambda b, h: (n * C_HEADS + h, b, 0))
    vec = pl.BlockSpec((1, HEAD_DIM), lambda b, h: (0, 0))
    return pl.pallas_call(
        functools.partial(_mixer_c_kernel, seq=seq),
        grid=(batch, C_HEADS),
        in_specs=[pl.BlockSpec(memory_space=pltpu.SMEM),
                  slab(0), slab(1), slab(2), vec, vec],
        out_specs=pl.BlockSpec((None, seq, LANES), lambda b, h: (h, b, 0)),
        out_shape=jax.ShapeDtypeStruct((C_HEADS, t, LANES), BF16),
        scratch_shapes=[
            pltpu.VMEM((seq, HEAD_DIM), F32),
            pltpu.VMEM((seq, HEAD_DIM), F32),
            pltpu.VMEM((seq, HEAD_DIM), F32),
            pltpu.VMEM((len(C_BRANCHES), seq, HEAD_DIM), F32),
            pltpu.VMEM((len(C_BRANCHES), seq, LANES), F32),
        ],
        compiler_params=_params("parallel", "arbitrary"),
        name="mixer_c",
    )(slopes, proj, proj, proj, q_g, k_g)


def _out_proj_kernel(x_ref, ya_ref, yb_ref, yc_ref, w_ref, g_ref, xo_ref, ho_ref):
    mix = jnp.concatenate([ya_ref[i] for i in range(A_GROUPS)]
                          + [yb_ref[i] for i in range(B_HEADS)]
                          + [yc_ref[i] for i in range(C_HEADS)], axis=1)
    xn = x_ref[...] + jnp.dot(mix, w_ref[...].astype(BF16), preferred_element_type=F32)
    xo_ref[...] = xn
    ho_ref[...] = _rms(xn, g_ref[...]).astype(BF16)


def _out_proj(x2, ya, yb, yc, w, gain, layer, *, tm=512):
    t = x2.shape[0]
    return pl.pallas_call(
        _out_proj_kernel,
        grid=(t // tm,),
        in_specs=[
            pl.BlockSpec((tm, D_MODEL), lambda i: (i, 0)),
            pl.BlockSpec((A_GROUPS, tm, LANES), lambda i: (0, i, 0)),
            pl.BlockSpec((B_HEADS, tm, LANES), lambda i: (0, i, 0)),
            pl.BlockSpec((C_HEADS, tm, LANES), lambda i: (0, i, 0)),
            pl.BlockSpec((None, D_MODEL, D_MODEL), lambda i: (layer, 0, 0), pipeline_mode=pl.Buffered(1)),
            pl.BlockSpec((None, 1, D_MODEL), lambda i: (layer, 0, 0)),
        ],
        out_specs=[
            pl.BlockSpec((tm, D_MODEL), lambda i: (i, 0)),
            pl.BlockSpec((tm, D_MODEL), lambda i: (i, 0)),
        ],
        out_shape=[
            jax.ShapeDtypeStruct((t, D_MODEL), F32),
            jax.ShapeDtypeStruct((t, D_MODEL), BF16),
        ],
        compiler_params=_params("parallel"),
        name="out_proj",
    )(x2, ya, yb, yc, w, gain)


def _ffn_kernel(h_ref, x_ref, wg_ref, wu_ref, wd_ref, o_ref):
    @pl.when(pl.program_id(1) == 0)
    def _():
        o_ref[...] = x_ref[...]

    h = h_ref[...]
    gt = jnp.dot(h, wg_ref[...], preferred_element_type=F32)
    up = jnp.dot(h, wu_ref[...], preferred_element_type=F32)
    act = (_silu(gt) * up).astype(BF16)
    o_ref[...] += jnp.dot(act, wd_ref[...], preferred_element_type=F32)


def _ffn(h2, x2, w_gate_up, w_down, layer, *, tm=512, tf=FFN_TILE):
    t = x2.shape[0]
    nf = FFN_HIDDEN // tf
    return pl.pallas_call(
        _ffn_kernel,
        grid=(t // tm, nf),
        in_specs=[
            pl.BlockSpec((tm, D_MODEL), lambda i, j: (i, 0)),
            pl.BlockSpec((tm, D_MODEL), lambda i, j: (i, 0)),
            pl.BlockSpec((None, D_MODEL, tf), lambda i, j: (layer, 0, j)),
            pl.BlockSpec((None, D_MODEL, tf), lambda i, j: (layer, 0, nf + j)),
            pl.BlockSpec((None, tf, D_MODEL), lambda i, j: (layer, j, 0)),
        ],
        out_specs=pl.BlockSpec((tm, D_MODEL), lambda i, j: (i, 0)),
        out_shape=jax.ShapeDtypeStruct((t, D_MODEL), F32),
        compiler_params=_params("parallel", "arbitrary"),
        name="ffn",
    )(h2, x2, w_gate_up, w_gate_up, w_down)


def kernel(x, norm1_g, w_in, sgu_norm_g, w_spatial, b_spatial, conv_w, a_log, dt_bias, o_norm_g,
           q_norm_g, k_norm_g, w_out, norm2_g, w_gate_up, w_down):
    batch, seq, _ = x.shape
    depth = w_in.shape[0]
    t = batch * seq
    slopes = jnp.exp2(-8.0 * (jnp.arange(C_HEADS, dtype=F32) + 1.0) / C_HEADS)
    w_in_t = _cast_w_in_t(w_in)
    w_c = w_in_t[:, OFF_C:, :]
    w_small = jnp.pad(w_in_t[:, OFF_BETA:OFF_C, :], ((0, 0), (0, SMALL_ROWS - 2 * B_HEADS), (0, 0)))
    w_out_b = w_out
    w_gate_up_b = w_gate_up.astype(BF16)
    w_down_b = w_down.astype(BF16)
    conv_w3 = conv_w.reshape(depth, B_CONV, 3 * B_HEADS, HEAD_DIM).transpose(0, 2, 1, 3)
    norm1 = norm1_g.reshape(depth, 1, D_MODEL)
    norm2 = norm2_g.reshape(depth, 1, D_MODEL)

    x2 = x.reshape(t, D_MODEL)
    for l in range(depth):
        pa, pb, pc, small3 = _in_proj(x2, norm1, w_in_t, w_c, w_small, l)
        ya = _mixer_a(pa, sgu_norm_g[l], w_spatial[l], b_spatial[l])
        yb = _mixer_b(pb, small3, conv_w3[l], a_log[l], dt_bias[l], o_norm_g[l].reshape(1, HEAD_DIM),
                      batch=batch, seq=seq)
        yc = _mixer_c(pc, slopes, q_norm_g[l].reshape(1, HEAD_DIM), k_norm_g[l].reshape(1, HEAD_DIM),
                      batch=batch, seq=seq)
        x2, h2 = _out_proj(x2, ya, yb, yc, w_out_b, norm2, l)
        x2 = _ffn(h2, x2, w_gate_up_b, w_down_b, l)
    return x2.reshape(batch, seq, D_MODEL)
```
